```python
import math
import jax, jax.numpy as jnp
from jax import lax
import numpy as np

D_MODEL = 2048
BATCH = 8
SEQ = 8192
DEPTH = 4

CHUNK = 64
N_MEM = 256
N_MIXERS = 2
N_A = (DEPTH + 1) // 2
N_B = DEPTH // 2
HEAD_DIM = 128
MEM_W = D_MODEL // 4
MEM_HEADS = MEM_W // HEAD_DIM
TOK_W = D_MODEL - MEM_W
SB_HEADS = TOK_W // HEAD_DIM
SB_BLOCK = 128
S5_GROUP = 16
S5_GROUPS = TOK_W // S5_GROUP
S5_STATE = 64
FFN_DIM = 256 * math.ceil(8 * D_MODEL / 3 / 256)
EPS = 1e-6

kernel_name = "hybrid_stickbreak_s5_macaron_memory_trunk"


def rms_norm(x, g):
    x32 = x.astype(jnp.float32)
    y = x32 * lax.rsqrt(jnp.mean(x32 * x32, axis=-1, keepdims=True) + EPS) * g.astype(jnp.float32)
    return y.astype(x.dtype)


def swiglu_ffn(h, w_gu, w_down):
    g, u = jnp.split(h @ w_gu, 2, axis=-1)
    return (jax.nn.silu(g) * u) @ w_down


def stick_breaking_attention(q, k, v):
    B, L, H, d = q.shape
    q, k, v = (a.transpose(0, 2, 1, 3) for a in (q, k, v))
    scale = 1.0 / math.sqrt(d)
    outs = []
    for qs in range(0, L, SB_BLOCK):
        qe = qs + SB_BLOCK
        z = jnp.einsum("bhqd,bhkd->bhqk", q[:, :, qs:qe], k[:, :, :qe]).astype(jnp.float32) * scale
        t_idx = qs + jnp.arange(SB_BLOCK)[:, None]
        s_idx = jnp.arange(qe)[None, :]
        before = s_idx < t_idx
        log_not = jnp.where(before, jax.nn.log_sigmoid(-z), 0.0)
        between = lax.cumsum(log_not, axis=3, reverse=True) - log_not
        w = jnp.where(before, jnp.exp(jax.nn.log_sigmoid(z) + between), 0.0)
        outs.append(jnp.einsum("bhqk,bhkd->bhqd", w.astype(v.dtype), v[:, :, :qe]))
    o = jnp.concatenate(outs, axis=2)
    return o.transpose(0, 2, 1, 3).reshape(B, L, H * d)


def _ssm_combine(left, right):
    a_l, b_l = left
    a_r, b_r = right
    return (a_r * a_l, a_r * b_l + b_r)


def s5_glu(u, log_dt, a_re, a_im, b_re, b_im, c_re, c_im, d_skip, w_glu):
    B, L, _ = u.shape
    f32 = jnp.float32
    dt = jnp.exp(log_dt.astype(f32))[:, None]
    lam = lax.complex(a_re.astype(f32), a_im.astype(f32))
    a_bar = jnp.exp(lam * dt)
    b = lax.complex(b_re.astype(f32), b_im.astype(f32))
    b_bar = ((a_bar - 1.0) / lam)[..., None] * b
    c = lax.complex(c_re.astype(f32), c_im.astype(f32))
    uc = u.astype(f32).reshape(B, L, S5_GROUPS, S5_GROUP)
    bu = jnp.einsum("gnc,blgc->blgn", b_bar, uc.astype(jnp.complex64))
    a_elems = jnp.broadcast_to(a_bar[None, None], bu.shape)
    _, states = lax.associative_scan(_ssm_combine, (a_elems, bu), axis=1)
    y = jnp.einsum("gcn,blgn->blgc", c, states).real + d_skip.astype(f32).reshape(S5_GROUPS, S5_GROUP) * uc
    y = jax.nn.gelu(y.reshape(B, L, TOK_W))
    y = y * jax.nn.sigmoid(y @ w_glu.astype(f32))
    return y.astype(u.dtype)


def memory_cross_attention(q, mem_h, w_mem_kv, q_gain, k_gain):
    B, L, _ = q.shape
    M = mem_h.shape[1]
    k, v = jnp.split(mem_h @ w_mem_kv, 2, axis=-1)
    q = rms_norm(q.reshape(B, L, MEM_HEADS, HEAD_DIM), q_gain)
    k = rms_norm(k.reshape(B, M, MEM_HEADS, HEAD_DIM), k_gain)
    v = v.reshape(B, M, MEM_HEADS, HEAD_DIM)
    s = jnp.einsum("blhd,bmhd->bhlm", q, k).astype(jnp.float32) / math.sqrt(HEAD_DIM)
    p = jax.nn.softmax(s, axis=-1).astype(v.dtype)
    o = jnp.einsum("bhlm,bmhd->blhd", p, v)
    return o.reshape(B, L, MEM_W)


def _fwd_setup_inputs(seed: int = 0) -> dict:
    key = jax.random.key(seed)
    ks = jax.random.split(key, 32)
    f32 = jnp.float32

    def nrm(k, shape, fan_in):
        return jax.random.normal(k, shape, f32) * fan_in ** -0.5

    def gain(k, shape):
        return 1.0 + 0.02 * jax.random.normal(k, shape, f32)

    G, N, C = S5_GROUPS, S5_STATE, S5_GROUP
    return dict(
        x=jax.random.normal(ks[0], (BATCH, SEQ, D_MODEL), f32),
        mem=jax.random.normal(ks[1], (BATCH, N_MEM, D_MODEL), f32),
        ffn1_norm=gain(ks[2], (DEPTH, D_MODEL)),
        ffn1_w_gu=nrm(ks[3], (DEPTH, D_MODEL, 2 * FFN_DIM), D_MODEL),
        ffn1_w_down=nrm(ks[4], (DEPTH, FFN_DIM, D_MODEL), FFN_DIM),
        mix_norm=gain(ks[5], (DEPTH, D_MODEL)),
        mem_norm=gain(ks[6], (DEPTH, D_MODEL)),
        w_mem_kv=nrm(ks[7], (DEPTH, D_MODEL, 2 * MEM_W), D_MODEL),
        xq_norm=gain(ks[8], (DEPTH, HEAD_DIM)),
        xk_norm=gain(ks[9], (DEPTH, HEAD_DIM)),
        w_out=nrm(ks[10], (DEPTH, TOK_W + MEM_W, D_MODEL), TOK_W + MEM_W),
        ffn2_norm=gain(ks[11], (DEPTH, D_MODEL)),
        ffn2_w_gu=nrm(ks[12], (DEPTH, D_MODEL, 2 * FFN_DIM), D_MODEL),
        ffn2_w_down=nrm(ks[13], (DEPTH, FFN_DIM, D_MODEL), FFN_DIM),
        sb_w_in=nrm(ks[14], (N_A, D_MODEL, 3 * TOK_W + MEM_W), D_MODEL),
        s5_w_in=nrm(ks[15], (N_B, D_MODEL, TOK_W + MEM_W), D_MODEL),
        s5_log_dt=jax.random.uniform(ks[16], (N_B, G), f32, math.log(1e-3), math.log(1e-1)),
        s5_a_re=-0.5 + 0.01 * jax.random.normal(ks[17], (N_B, G, N), f32),
        s5_a_im=jnp.pi * jnp.arange(N, dtype=f32) + 0.01 * jax.random.normal(ks[18], (N_B, G, N), f32),
        s5_b_re=nrm(ks[19], (N_B, G, N, C), 2 * C),
        s5_b_im=nrm(ks[20], (N_B, G, N, C), 2 * C),
        s5_c_re=nrm(ks[21], (N_B, G, C, N), 2 * N),
        s5_c_im=nrm(ks[22], (N_B, G, C, N), 2 * N),
        s5_d=jax.random.normal(ks[23], (N_B, TOK_W), f32),
        s5_w_glu=nrm(ks[24], (N_B, TOK_W, TOK_W), TOK_W),
    )


def _fwd_reference(x, mem, ffn1_norm, ffn1_w_gu, ffn1_w_down, mix_norm, mem_norm, w_mem_kv,
              xq_norm, xk_norm, w_out, ffn2_norm, ffn2_w_gu, ffn2_w_down, sb_w_in,
              s5_w_in, s5_log_dt, s5_a_re, s5_a_im, s5_b_re, s5_b_im, s5_c_re, s5_c_im,
              s5_d, s5_w_glu):
    B, L, _ = x.shape
    for i in range(DEPTH):
        x = x + 0.5 * swiglu_ffn(rms_norm(x, ffn1_norm[i]), ffn1_w_gu[i], ffn1_w_down[i])
        h = rms_norm(x, mix_norm[i])
        j = i // N_MIXERS
        if i % N_MIXERS == 0:
            q, k, v, q_mem = jnp.split(h @ sb_w_in[j], [TOK_W, 2 * TOK_W, 3 * TOK_W], axis=-1)
            shp = (B, L, SB_HEADS, HEAD_DIM)
            tok = stick_breaking_attention(q.reshape(shp), k.reshape(shp), v.reshape(shp))
        else:
            u, q_mem = jnp.split(h @ s5_w_in[j], [TOK_W], axis=-1)
            tok = s5_glu(u, s5_log_dt[j], s5_a_re[j], s5_a_im[j], s5_b_re[j], s5_b_im[j],
                         s5_c_re[j], s5_c_im[j], s5_d[j], s5_w_glu[j])
        mem_h = rms_norm(mem, mem_norm[i])
        cross = memory_cross_attention(q_mem, mem_h, w_mem_kv[i], xq_norm[i], xk_norm[i])
        x = x + jnp.concatenate([tok, cross], axis=-1) @ w_out[i]
        x = x + 0.5 * swiglu_ffn(rms_norm(x, ffn2_norm[i]), ffn2_w_gu[i], ffn2_w_down[i])
    return x


import jax as _jax
import jax.numpy as _jnp

TWIN_FORMAT = 'train_step'
FWD_PARAMS = ['x', 'mem', 'ffn1_norm', 'ffn1_w_gu', 'ffn1_w_down', 'mix_norm', 'mem_norm', 'w_mem_kv', 'xq_norm', 'xk_norm', 'w_out', 'ffn2_norm', 'ffn2_w_gu', 'ffn2_w_down', 'sb_w_in', 's5_w_in', 's5_log_dt', 's5_a_re', 's5_a_im', 's5_b_re', 's5_b_im', 's5_c_re', 's5_c_im', 's5_d', 's5_w_glu']
TWIN_WEIGHTS = ['ffn1_norm', 'ffn1_w_gu', 'ffn1_w_down', 'mix_norm', 'mem_norm', 'w_mem_kv', 'xq_norm', 'xk_norm', 'w_out', 'ffn2_norm', 'ffn2_w_gu', 'ffn2_w_down', 'sb_w_in', 's5_w_in', 's5_log_dt', 's5_a_re', 's5_a_im', 's5_b_re', 's5_b_im', 's5_c_re', 's5_c_im', 's5_d', 's5_w_glu']
TWIN_DIFF_INPUT = 'x'
TWIN_INPUTS = ['x', 'mem', 'ffn1_norm', 'ffn1_w_gu', 'ffn1_w_down', 'mix_norm', 'mem_norm', 'w_mem_kv', 'xq_norm', 'xk_norm', 'w_out', 'ffn2_norm', 'ffn2_w_gu', 'ffn2_w_down', 'sb_w_in', 's5_w_in', 's5_log_dt', 's5_a_re', 's5_a_im', 's5_b_re', 's5_b_im', 's5_c_re', 's5_c_im', 's5_d', 's5_w_glu', 'loss_target', 'm_ffn1_norm', 'm_ffn1_w_gu', 'm_ffn1_w_down', 'm_mix_norm', 'm_mem_norm', 'm_w_mem_kv', 'm_xq_norm', 'm_xk_norm', 'm_w_out', 'm_ffn2_norm', 'm_ffn2_w_gu', 'm_ffn2_w_down', 'm_sb_w_in', 'm_s5_w_in', 'm_s5_log_dt', 'm_s5_a_re', 'm_s5_a_im', 'm_s5_b_re', 'm_s5_b_im', 'm_s5_c_re', 'm_s5_c_im', 'm_s5_d', 'm_s5_w_glu', 'v_ffn1_norm', 'v_ffn1_w_gu', 'v_ffn1_w_down', 'v_mix_norm', 'v_mem_norm', 'v_w_mem_kv', 'v_xq_norm', 'v_xk_norm', 'v_w_out', 'v_ffn2_norm', 'v_ffn2_w_gu', 'v_ffn2_w_down', 'v_sb_w_in', 'v_s5_w_in', 'v_s5_log_dt', 'v_s5_a_re', 'v_s5_a_im', 'v_s5_b_re', 'v_s5_b_im', 'v_s5_c_re', 'v_s5_c_im', 'v_s5_d', 'v_s5_w_glu']
TWIN_OUTPUTS = ['loss', 'grad_x', 'grad_ffn1_norm', 'grad_ffn1_w_gu', 'grad_ffn1_w_down', 'grad_mix_norm', 'grad_mem_norm', 'grad_w_mem_kv', 'grad_xq_norm', 'grad_xk_norm', 'grad_w_out', 'grad_ffn2_norm', 'grad_ffn2_w_gu', 'grad_ffn2_w_down', 'grad_sb_w_in', 'grad_s5_w_in', 'grad_s5_log_dt', 'grad_s5_a_re', 'grad_s5_a_im', 'grad_s5_b_re', 'grad_s5_b_im', 'grad_s5_c_re', 'grad_s5_c_im', 'grad_s5_d', 'grad_s5_w_glu', 'delta_ffn1_norm', 'delta_ffn1_w_gu', 'delta_ffn1_w_down', 'delta_mix_norm', 'delta_mem_norm', 'delta_w_mem_kv', 'delta_xq_norm', 'delta_xk_norm', 'delta_w_out', 'delta_ffn2_norm', 'delta_ffn2_w_gu', 'delta_ffn2_w_down', 'delta_sb_w_in', 'delta_s5_w_in', 'delta_s5_log_dt', 'delta_s5_a_re', 'delta_s5_a_im', 'delta_s5_b_re', 'delta_s5_b_im', 'delta_s5_c_re', 'delta_s5_c_im', 'delta_s5_d', 'delta_s5_w_glu', 'new_m_ffn1_norm', 'new_m_ffn1_w_gu', 'new_m_ffn1_w_down', 'new_m_mix_norm', 'new_m_mem_norm', 'new_m_w_mem_kv', 'new_m_xq_norm', 'new_m_xk_norm', 'new_m_w_out', 'new_m_ffn2_norm', 'new_m_ffn2_w_gu', 'new_m_ffn2_w_down', 'new_m_sb_w_in', 'new_m_s5_w_in', 'new_m_s5_log_dt', 'new_m_s5_a_re', 'new_m_s5_a_im', 'new_m_s5_b_re', 'new_m_s5_b_im', 'new_m_s5_c_re', 'new_m_s5_c_im', 'new_m_s5_d', 'new_m_s5_w_glu', 'new_v_ffn1_norm', 'new_v_ffn1_w_gu', 'new_v_ffn1_w_down', 'new_v_mix_norm', 'new_v_mem_norm', 'new_v_w_mem_kv', 'new_v_xq_norm', 'new_v_xk_norm', 'new_v_w_out', 'new_v_ffn2_norm', 'new_v_ffn2_w_gu', 'new_v_ffn2_w_down', 'new_v_sb_w_in', 'new_v_s5_w_in', 'new_v_s5_log_dt', 'new_v_s5_a_re', 'new_v_s5_a_im', 'new_v_s5_b_re', 'new_v_s5_b_im', 'new_v_s5_c_re', 'new_v_s5_c_im', 'new_v_s5_d', 'new_v_s5_w_glu']
TWIN_LEAF_KINDS = {'loss': 'loss', 'grad_x': 'grad_x', 'grad_ffn1_norm': 'grad_w', 'grad_ffn1_w_gu': 'grad_w', 'grad_ffn1_w_down': 'grad_w', 'grad_mix_norm': 'grad_w', 'grad_mem_norm': 'grad_w', 'grad_w_mem_kv': 'grad_w', 'grad_xq_norm': 'grad_w', 'grad_xk_norm': 'grad_w', 'grad_w_out': 'grad_w', 'grad_ffn2_norm': 'grad_w', 'grad_ffn2_w_gu': 'grad_w', 'grad_ffn2_w_down': 'grad_w', 'grad_sb_w_in': 'grad_w', 'grad_s5_w_in': 'grad_w', 'grad_s5_log_dt': 'grad_w', 'grad_s5_a_re': 'grad_w', 'grad_s5_a_im': 'grad_w', 'grad_s5_b_re': 'grad_w', 'grad_s5_b_im': 'grad_w', 'grad_s5_c_re': 'grad_w', 'grad_s5_c_im': 'grad_w', 'grad_s5_d': 'grad_w', 'grad_s5_w_glu': 'grad_w', 'delta_ffn1_norm': 'delta_w', 'delta_ffn1_w_gu': 'delta_w', 'delta_ffn1_w_down': 'delta_w', 'delta_mix_norm': 'delta_w', 'delta_mem_norm': 'delta_w', 'delta_w_mem_kv': 'delta_w', 'delta_xq_norm': 'delta_w', 'delta_xk_norm': 'delta_w', 'delta_w_out': 'delta_w', 'delta_ffn2_norm': 'delta_w', 'delta_ffn2_w_gu': 'delta_w', 'delta_ffn2_w_down': 'delta_w', 'delta_sb_w_in': 'delta_w', 'delta_s5_w_in': 'delta_w', 'delta_s5_log_dt': 'delta_w', 'delta_s5_a_re': 'delta_w', 'delta_s5_a_im': 'delta_w', 'delta_s5_b_re': 'delta_w', 'delta_s5_b_im': 'delta_w', 'delta_s5_c_re': 'delta_w', 'delta_s5_c_im': 'delta_w', 'delta_s5_d': 'delta_w', 'delta_s5_w_glu': 'delta_w', 'new_m_ffn1_norm': 'new_m', 'new_m_ffn1_w_gu': 'new_m', 'new_m_ffn1_w_down': 'new_m', 'new_m_mix_norm': 'new_m', 'new_m_mem_norm': 'new_m', 'new_m_w_mem_kv': 'new_m', 'new_m_xq_norm': 'new_m', 'new_m_xk_norm': 'new_m', 'new_m_w_out': 'new_m', 'new_m_ffn2_norm': 'new_m', 'new_m_ffn2_w_gu': 'new_m', 'new_m_ffn2_w_down': 'new_m', 'new_m_sb_w_in': 'new_m', 'new_m_s5_w_in': 'new_m', 'new_m_s5_log_dt': 'new_m', 'new_m_s5_a_re': 'new_m', 'new_m_s5_a_im': 'new_m', 'new_m_s5_b_re': 'new_m', 'new_m_s5_b_im': 'new_m', 'new_m_s5_c_re': 'new_m', 'new_m_s5_c_im': 'new_m', 'new_m_s5_d': 'new_m', 'new_m_s5_w_glu': 'new_m', 'new_v_ffn1_norm': 'new_v', 'new_v_ffn1_w_gu': 'new_v', 'new_v_ffn1_w_down': 'new_v', 'new_v_mix_norm': 'new_v', 'new_v_mem_norm': 'new_v', 'new_v_w_mem_kv': 'new_v', 'new_v_xq_norm': 'new_v', 'new_v_xk_norm': 'new_v', 'new_v_w_out': 'new_v', 'new_v_ffn2_norm': 'new_v', 'new_v_ffn2_w_gu': 'new_v', 'new_v_ffn2_w_down': 'new_v', 'new_v_sb_w_in': 'new_v', 'new_v_s5_w_in': 'new_v', 'new_v_s5_log_dt': 'new_v', 'new_v_s5_a_re': 'new_v', 'new_v_s5_a_im': 'new_v', 'new_v_s5_b_re': 'new_v', 'new_v_s5_b_im': 'new_v', 'new_v_s5_c_re': 'new_v', 'new_v_s5_c_im': 'new_v', 'new_v_s5_d': 'new_v', 'new_v_s5_w_glu': 'new_v'}


def _forward(args):
    return _fwd_reference(*[args[k] for k in FWD_PARAMS])


def _output_shape():
    def fwd():
        inp = _fwd_setup_inputs(0)
        return _fwd_reference(*[inp[k] for k in FWD_PARAMS])
    out = _jax.eval_shape(fwd)
    return out.shape, out.dtype

N_MICROBATCH = 1
ADAM_LR = 0.001
ADAM_B1 = 0.9
ADAM_B2 = 0.999
ADAM_EPS = 1e-08
ADAM_WD = 0.01
ADAM_STEP = 10
PER_EXAMPLE_BATCH_AXIS = {'x': 0, 'mem': 0, 'loss_target': 0}
SHARED_INPUTS = []
_WEIGHT_DTYPES = {'ffn1_norm': _jnp.float32, 'ffn1_w_gu': _jnp.float32, 'ffn1_w_down': _jnp.float32, 'mix_norm': _jnp.float32, 'mem_norm': _jnp.float32, 'w_mem_kv': _jnp.float32, 'xq_norm': _jnp.float32, 'xk_norm': _jnp.float32, 'w_out': _jnp.float32, 'ffn2_norm': _jnp.float32, 'ffn2_w_gu': _jnp.float32, 'ffn2_w_down': _jnp.float32, 'sb_w_in': _jnp.float32, 's5_w_in': _jnp.float32, 's5_log_dt': _jnp.float32, 's5_a_re': _jnp.float32, 's5_a_im': _jnp.float32, 's5_b_re': _jnp.float32, 's5_b_im': _jnp.float32, 's5_c_re': _jnp.float32, 's5_c_im': _jnp.float32, 's5_d': _jnp.float32, 's5_w_glu': _jnp.float32}
MOMENT_SCALE = {'ffn1_norm': 6.065897e+00, 'ffn1_w_gu': 8.793468e-02, 'ffn1_w_down': 1.470823e-01, 'mix_norm': 9.773400e+00, 'mem_norm': 1.227356e-01, 'w_mem_kv': 1.317121e-01, 'xq_norm': 1.108743e+00, 'xk_norm': 1.108372e+00, 'w_out': 8.641957e-01, 'ffn2_norm': 6.101217e+00, 'ffn2_w_gu': 8.433685e-02, 'ffn2_w_down': 1.382954e-01, 'sb_w_in': 3.583196e-01, 's5_w_in': 4.565025e-01, 's5_log_dt': 4.902695e+00, 's5_a_re': 2.660042e-02, 's5_a_im': 3.327533e-02, 's5_b_re': 2.838993e-02, 's5_b_im': 2.536854e-02, 's5_c_re': 5.319826e-02, 's5_c_im': 5.153748e-02, 's5_d': 5.173186e+00, 's5_w_glu': 1.059485e+00}


def _to_microbatches(a, axis):
    t = _jnp.moveaxis(a, axis, 0)
    t = t.reshape((N_MICROBATCH, t.shape[0] // N_MICROBATCH) + t.shape[1:])
    return _jnp.moveaxis(t, 1, axis + 1)


def setup_inputs(seed: int = 0) -> dict:
    inp = _fwd_setup_inputs(seed)
    key = _jax.random.fold_in(_jax.random.key(seed), 7919)
    shape, _ = _output_shape()
    out = dict(inp)
    out["loss_target"] = _jax.random.normal(_jax.random.fold_in(key, 0), shape, _jnp.float32)
    for i, name in enumerate(TWIN_WEIGHTS):
        w = inp[name].astype(_jnp.float32)
        if MOMENT_SCALE is None:
            s = _jnp.sqrt(_jnp.mean(_jnp.square(w)) + 1e-30)
        else:
            s = MOMENT_SCALE[name]
        km, kv = _jax.random.split(_jax.random.fold_in(key, i + 1))
        out[name] = w
        out["m_" + name] = s * _jax.random.normal(km, w.shape, _jnp.float32)
        out["v_" + name] = (s * s) * _jax.random.uniform(kv, w.shape, _jnp.float32, 0.5, 1.5)
    if N_MICROBATCH > 1:
        for name, axis in PER_EXAMPLE_BATCH_AXIS.items():
            out[name] = _to_microbatches(out[name], axis)
    return {'x': out['x'], 'mem': out['mem'], 'ffn1_norm': out['ffn1_norm'], 'ffn1_w_gu': out['ffn1_w_gu'], 'ffn1_w_down': out['ffn1_w_down'], 'mix_norm': out['mix_norm'], 'mem_norm': out['mem_norm'], 'w_mem_kv': out['w_mem_kv'], 'xq_norm': out['xq_norm'], 'xk_norm': out['xk_norm'], 'w_out': out['w_out'], 'ffn2_norm': out['ffn2_norm'], 'ffn2_w_gu': out['ffn2_w_gu'], 'ffn2_w_down': out['ffn2_w_down'], 'sb_w_in': out['sb_w_in'], 's5_w_in': out['s5_w_in'], 's5_log_dt': out['s5_log_dt'], 's5_a_re': out['s5_a_re'], 's5_a_im': out['s5_a_im'], 's5_b_re': out['s5_b_re'], 's5_b_im': out['s5_b_im'], 's5_c_re': out['s5_c_re'], 's5_c_im': out['s5_c_im'], 's5_d': out['s5_d'], 's5_w_glu': out['s5_w_glu'], 'loss_target': out['loss_target'], 'm_ffn1_norm': out['m_ffn1_norm'], 'm_ffn1_w_gu': out['m_ffn1_w_gu'], 'm_ffn1_w_down': out['m_ffn1_w_down'], 'm_mix_norm': out['m_mix_norm'], 'm_mem_norm': out['m_mem_norm'], 'm_w_mem_kv': out['m_w_mem_kv'], 'm_xq_norm': out['m_xq_norm'], 'm_xk_norm': out['m_xk_norm'], 'm_w_out': out['m_w_out'], 'm_ffn2_norm': out['m_ffn2_norm'], 'm_ffn2_w_gu': out['m_ffn2_w_gu'], 'm_ffn2_w_down': out['m_ffn2_w_down'], 'm_sb_w_in': out['m_sb_w_in'], 'm_s5_w_in': out['m_s5_w_in'], 'm_s5_log_dt': out['m_s5_log_dt'], 'm_s5_a_re': out['m_s5_a_re'], 'm_s5_a_im': out['m_s5_a_im'], 'm_s5_b_re': out['m_s5_b_re'], 'm_s5_b_im': out['m_s5_b_im'], 'm_s5_c_re': out['m_s5_c_re'], 'm_s5_c_im': out['m_s5_c_im'], 'm_s5_d': out['m_s5_d'], 'm_s5_w_glu': out['m_s5_w_glu'], 'v_ffn1_norm': out['v_ffn1_norm'], 'v_ffn1_w_gu': out['v_ffn1_w_gu'], 'v_ffn1_w_down': out['v_ffn1_w_down'], 'v_mix_norm': out['v_mix_norm'], 'v_mem_norm': out['v_mem_norm'], 'v_w_mem_kv': out['v_w_mem_kv'], 'v_xq_norm': out['v_xq_norm'], 'v_xk_norm': out['v_xk_norm'], 'v_w_out': out['v_w_out'], 'v_ffn2_norm': out['v_ffn2_norm'], 'v_ffn2_w_gu': out['v_ffn2_w_gu'], 'v_ffn2_w_down': out['v_ffn2_w_down'], 'v_sb_w_in': out['v_sb_w_in'], 'v_s5_w_in': out['v_s5_w_in'], 'v_s5_log_dt': out['v_s5_log_dt'], 'v_s5_a_re': out['v_s5_a_re'], 'v_s5_a_im': out['v_s5_a_im'], 'v_s5_b_re': out['v_s5_b_re'], 'v_s5_b_im': out['v_s5_b_im'], 'v_s5_c_re': out['v_s5_c_re'], 'v_s5_c_im': out['v_s5_c_im'], 'v_s5_d': out['v_s5_d'], 'v_s5_w_glu': out['v_s5_w_glu']}


def _loss(weights, diff, rest, loss_target):
    with _jax.named_scope("forward"):
        args = {**rest, TWIN_DIFF_INPUT: diff, **{k: w.astype(_WEIGHT_DTYPES[k]) for k, w in weights.items()}}
        y = _forward(args)
    with _jax.named_scope("loss_head"):
        err = _jnp.square(y.astype(_jnp.float32) - loss_target)
        return 0.5 * _jnp.sum(_jnp.mean(err, axis=-1)) if err.ndim else 0.5 * err


def _adamw(w, g, m, v):
    m = ADAM_B1 * m + (1.0 - ADAM_B1) * g
    v = ADAM_B2 * v + (1.0 - ADAM_B2) * _jnp.square(g)
    m_hat = m / (1.0 - ADAM_B1 ** ADAM_STEP)
    v_hat = v / (1.0 - ADAM_B2 ** ADAM_STEP)
    delta = -ADAM_LR * (m_hat / (_jnp.sqrt(v_hat) + ADAM_EPS) + ADAM_WD * w)
    return delta, m, v


def reference(x, mem, ffn1_norm, ffn1_w_gu, ffn1_w_down, mix_norm, mem_norm, w_mem_kv, xq_norm, xk_norm, w_out, ffn2_norm, ffn2_w_gu, ffn2_w_down, sb_w_in, s5_w_in, s5_log_dt, s5_a_re, s5_a_im, s5_b_re, s5_b_im, s5_c_re, s5_c_im, s5_d, s5_w_glu, loss_target, m_ffn1_norm, m_ffn1_w_gu, m_ffn1_w_down, m_mix_norm, m_mem_norm, m_w_mem_kv, m_xq_norm, m_xk_norm, m_w_out, m_ffn2_norm, m_ffn2_w_gu, m_ffn2_w_down, m_sb_w_in, m_s5_w_in, m_s5_log_dt, m_s5_a_re, m_s5_a_im, m_s5_b_re, m_s5_b_im, m_s5_c_re, m_s5_c_im, m_s5_d, m_s5_w_glu, v_ffn1_norm, v_ffn1_w_gu, v_ffn1_w_down, v_mix_norm, v_mem_norm, v_w_mem_kv, v_xq_norm, v_xk_norm, v_w_out, v_ffn2_norm, v_ffn2_w_gu, v_ffn2_w_down, v_sb_w_in, v_s5_w_in, v_s5_log_dt, v_s5_a_re, v_s5_a_im, v_s5_b_re, v_s5_b_im, v_s5_c_re, v_s5_c_im, v_s5_d, v_s5_w_glu):
    given = dict(x=x, mem=mem, ffn1_norm=ffn1_norm, ffn1_w_gu=ffn1_w_gu, ffn1_w_down=ffn1_w_down, mix_norm=mix_norm, mem_norm=mem_norm, w_mem_kv=w_mem_kv, xq_norm=xq_norm, xk_norm=xk_norm, w_out=w_out, ffn2_norm=ffn2_norm, ffn2_w_gu=ffn2_w_gu, ffn2_w_down=ffn2_w_down, sb_w_in=sb_w_in, s5_w_in=s5_w_in, s5_log_dt=s5_log_dt, s5_a_re=s5_a_re, s5_a_im=s5_a_im, s5_b_re=s5_b_re, s5_b_im=s5_b_im, s5_c_re=s5_c_re, s5_c_im=s5_c_im, s5_d=s5_d, s5_w_glu=s5_w_glu, loss_target=loss_target, m_ffn1_norm=m_ffn1_norm, m_ffn1_w_gu=m_ffn1_w_gu, m_ffn1_w_down=m_ffn1_w_down, m_mix_norm=m_mix_norm, m_mem_norm=m_mem_norm, m_w_mem_kv=m_w_mem_kv, m_xq_norm=m_xq_norm, m_xk_norm=m_xk_norm, m_w_out=m_w_out, m_ffn2_norm=m_ffn2_norm, m_ffn2_w_gu=m_ffn2_w_gu, m_ffn2_w_down=m_ffn2_w_down, m_sb_w_in=m_sb_w_in, m_s5_w_in=m_s5_w_in, m_s5_log_dt=m_s5_log_dt, m_s5_a_re=m_s5_a_re, m_s5_a_im=m_s5_a_im, m_s5_b_re=m_s5_b_re, m_s5_b_im=m_s5_b_im, m_s5_c_re=m_s5_c_re, m_s5_c_im=m_s5_c_im, m_s5_d=m_s5_d, m_s5_w_glu=m_s5_w_glu, v_ffn1_norm=v_ffn1_norm, v_ffn1_w_gu=v_ffn1_w_gu, v_ffn1_w_down=v_ffn1_w_down, v_mix_norm=v_mix_norm, v_mem_norm=v_mem_norm, v_w_mem_kv=v_w_mem_kv, v_xq_norm=v_xq_norm, v_xk_norm=v_xk_norm, v_w_out=v_w_out, v_ffn2_norm=v_ffn2_norm, v_ffn2_w_gu=v_ffn2_w_gu, v_ffn2_w_down=v_ffn2_w_down, v_sb_w_in=v_sb_w_in, v_s5_w_in=v_s5_w_in, v_s5_log_dt=v_s5_log_dt, v_s5_a_re=v_s5_a_re, v_s5_a_im=v_s5_a_im, v_s5_b_re=v_s5_b_re, v_s5_b_im=v_s5_b_im, v_s5_c_re=v_s5_c_re, v_s5_c_im=v_s5_c_im, v_s5_d=v_s5_d, v_s5_w_glu=v_s5_w_glu)
    weights = {n: given[n] for n in TWIN_WEIGHTS}
    shared = {n: given[n] for n in SHARED_INPUTS}
    per_example = {n: given[n] for n in ['x', 'mem']}
    grad_fn = _jax.value_and_grad(_loss, argnums=(0, 1))

    def one_microbatch(ex, loss_target):
        ex = dict(ex)
        diff = ex.pop(TWIN_DIFF_INPUT)
        return grad_fn(weights, diff, {**shared, **ex}, loss_target)

    if N_MICROBATCH == 1:
        loss, (grad_w, grad_x) = one_microbatch(per_example, given["loss_target"])
    else:
        def body(carry, xs):
            loss_sum, grad_sum = carry
            l_k, (gw_k, gx_k) = one_microbatch(xs[0], xs[1])
            with _jax.named_scope("update"):
                return (loss_sum + l_k, _jax.tree.map(_jnp.add, grad_sum, gw_k)), gx_k

        init = (_jnp.zeros((), _jnp.float32), _jax.tree.map(_jnp.zeros_like, weights))
        (loss, grad_w), grad_x = _jax.lax.scan(body, init, (per_example, given["loss_target"]))
    with _jax.named_scope("update"):
        delta_w, new_m, new_v = {}, {}, {}
        for n in TWIN_WEIGHTS:
            delta_w[n], new_m[n], new_v[n] = _adamw(weights[n], grad_w[n], given["m_" + n], given["v_" + n])
    return (loss, grad_x, *[grad_w[n] for n in TWIN_WEIGHTS], *[delta_w[n] for n in TWIN_WEIGHTS],
            *[new_m[n] for n in TWIN_WEIGHTS], *[new_v[n] for n in TWIN_WEIGHTS])
```

```python
import functools
import math

import jax
import jax.numpy as jnp
from jax import lax
from jax.experimental import pallas as pl
from jax.experimental.pallas import tpu as pltpu

F32 = jnp.float32
BF16 = jnp.bfloat16
MESH = pl.DeviceIdType.MESH

HEAD_DIM = 128
S5_GROUP = 16
S5_STATE = 64
S5_PACK = 8
EPS = 1e-6
ADAM_LR, ADAM_B1, ADAM_B2, ADAM_EPS, ADAM_WD, ADAM_STEP = 0.001, 0.9, 0.999, 1e-08, 0.01, 10
N_DEV = 8
VMEM_LIMIT_BYTES = 56 * 1024 * 1024
SB_TILE = 256
LANE = 128


def _params(*sem):
    if sem:
        return pltpu.CompilerParams(dimension_semantics=sem, vmem_limit_bytes=VMEM_LIMIT_BYTES)
    return pltpu.CompilerParams(vmem_limit_bytes=VMEM_LIMIT_BYTES)


def _pick(dim, target, align=LANE):
    if dim <= target:
        return dim
    best = None
    for t in range(align, target + 1, align):
        if dim % t == 0:
            best = t
    assert best is not None, (dim, target, align)
    return best


def _mm(a, b, *, ta=False, tb=False, out_dtype=F32, res=None, scale=None, name):
    if ta:
        K, M = a.shape
    else:
        M, K = a.shape
    if tb:
        N, Kb = b.shape
    else:
        Kb, N = b.shape
    assert K == Kb, (a.shape, b.shape, ta, tb)
    tm, tn, tk = _pick(M, 1024), _pick(N, 1024), _pick(K, 512)
    nk = K // tk
    dims = (((0 if ta else 1,), (1 if tb else 0,)), ((), ()))

    def body(*refs):
        if res is None:
            a_ref, b_ref, o_ref, acc = refs
            r_ref = None
        else:
            a_ref, b_ref, r_ref, o_ref, acc = refs
        k = pl.program_id(2)

        @pl.when(k == 0)
        def _():
            acc[...] = jnp.zeros_like(acc)

        acc[...] += lax.dot_general(a_ref[...].astype(BF16), b_ref[...].astype(BF16), dims,
                                    preferred_element_type=F32)

        @pl.when(k == nk - 1)
        def _():
            o = acc[...]
            if scale is not None:
                o = o * scale
            if r_ref is not None:
                o = r_ref[...].astype(F32) + o
            o_ref[...] = o.astype(o_ref.dtype)

    a_spec = pl.BlockSpec((tk, tm), lambda i, j, k: (k, i)) if ta else pl.BlockSpec((tm, tk), lambda i, j, k: (i, k))
    b_spec = pl.BlockSpec((tn, tk), lambda i, j, k: (j, k)) if tb else pl.BlockSpec((tk, tn), lambda i, j, k: (k, j))
    o_spec = pl.BlockSpec((tm, tn), lambda i, j, k: (i, j))
    in_specs, args = [a_spec, b_spec], [a, b]
    if res is not None:
        in_specs.append(o_spec)
        args.append(res)
    return pl.pallas_call(
        body, name=name, grid=(M // tm, N // tn, nk),
        in_specs=in_specs, out_specs=o_spec,
        out_shape=jax.ShapeDtypeStruct((M, N), out_dtype),
        scratch_shapes=[pltpu.VMEM((tm, tn), F32)],
        compiler_params=_params("parallel", "parallel", "arbitrary"),
    )(*args)


class _Whole:
    def __init__(self, a):
        self.a = a


def _ew(fn, ins, outs, accs=(), *, rows, cols, tl, tc=None, name):
    tc = cols if tc is None else tc
    nj, ni = cols // tc, rows // tl
    assert nj * tc == cols and ni * tl == rows, (rows, cols, tl, tc)
    in_specs, args = [], []
    for it in ins:
        if isinstance(it, _Whole):
            in_specs.append(pl.BlockSpec(it.a.shape, lambda j, i, _n=it.a.ndim: (0,) * _n))
            args.append(it.a)
        else:
            arr, ro, co = it if isinstance(it, tuple) else (it, 0, 0)
            in_specs.append(pl.BlockSpec((tl, tc), lambda j, i, _ro=ro, _co=co: (i + _ro, j + _co)))
            args.append(arr)
    out_shape = [jax.ShapeDtypeStruct((rows, c), dt) for c, dt in outs]
    out_specs = [pl.BlockSpec((tl, tc), lambda j, i: (i, j)) for _ in outs]
    for c, t in accs:
        out_shape.append(jax.ShapeDtypeStruct((1, c), F32))
        out_specs.append(pl.BlockSpec((1, t), lambda j, i: (0, j)))
    n_in, n_out = len(args), len(outs)

    def body(*refs):
        j, i = pl.program_id(0), pl.program_id(1)
        vals = fn(j, *[r[...] for r in refs[:n_in]])
        if not isinstance(vals, (tuple, list)):
            vals = (vals,)
        for r, v in zip(refs[n_in:n_in + n_out], vals[:n_out]):
            r[...] = v.astype(r.dtype)
        for r, v in zip(refs[n_in + n_out:], vals[n_out:]):
            @pl.when(i == 0)
            def _(r=r):
                r[...] = jnp.zeros_like(r)
            r[...] += v

    res = pl.pallas_call(
        body, name=name, grid=(nj, ni), in_specs=in_specs, out_specs=out_specs, out_shape=out_shape,
        compiler_params=_params("parallel", "arbitrary"),
    )(*args)
    return res


def _row_tile(rows, bytes_per_row, budget=10 * 1024 * 1024, align=16):
    cap = max(align, budget // max(1, bytes_per_row))
    best = None
    for t in range(align, min(rows, cap) + 1, align):
        if rows % t == 0:
            best = t
    if best is None:
        best = rows
    return best


def _cast_bf16(w2d, name):
    rows, cols = w2d.shape
    tl = _row_tile(rows, cols * 6)
    return _ew(lambda j, w: w, [w2d], [(cols, BF16)], rows=rows, cols=cols, tl=tl, name=name)[0]


def _rmsnorm_fwd(x, g, name, out_dtype=BF16):
    rows, cols = x.shape

    def fn(j, xb, gb):
        r = lax.rsqrt(jnp.mean(xb * xb, axis=-1, keepdims=True) + EPS)
        return (xb * r * gb,)

    tl = _row_tile(rows, cols * 12)
    return _ew(fn, [x, _Whole(g.reshape(1, cols))], [(cols, out_dtype)], rows=rows, cols=cols, tl=tl, name=name)[0]


def _rmsnorm_bwd(dh, x, g, dres, name):
    rows, cols = x.shape

    def fn(j, dhb, xb, gb, *rest):
        r = lax.rsqrt(jnp.mean(xb * xb, axis=-1, keepdims=True) + EPS)
        xh = xb * r
        dxh = dhb * gb
        dx = r * (dxh - xh * jnp.mean(dxh * xh, axis=-1, keepdims=True))
        if rest:
            dx = dx + rest[0]
        return dx, jnp.sum(dhb * xh, axis=0, keepdims=True)

    ins = [dh, x, _Whole(g.reshape(1, cols))] + ([dres] if dres is not None else [])
    tl = _row_tile(rows, cols * 24)
    dx, dg = _ew(fn, ins, [(cols, F32)], [(cols, cols)], rows=rows, cols=cols, tl=tl, name=name)
    return dx, dg


def _silu(a):
    return a * jax.nn.sigmoid(a)


def _swiglu_fwd(gu, name):
    rows, cols2 = gu.shape
    f = cols2 // 2
    tc = _pick(f, 1536)
    tl = _row_tile(rows, tc * 12)
    fn = lambda j, a, b: (_silu(a.astype(F32)) * b.astype(F32),)
    return _ew(fn, [gu, (gu, 0, f // tc)], [(f, BF16)], rows=rows, cols=f, tl=tl, tc=tc, name=name)[0]


def _swiglu_bwd(ds, gu, name):
    rows, cols2 = gu.shape
    f = cols2 // 2
    tc = _pick(f, 1536)
    nb = f // tc
    tl = _row_tile(rows, tc * 16)

    def body(ds_ref, a_ref, b_ref, o_ref):
        j = pl.program_id(0)
        a, b, d = a_ref[...].astype(F32), b_ref[...].astype(F32), ds_ref[...].astype(F32)
        sg = jax.nn.sigmoid(a)
        da = d * b * (sg * (1.0 + a * (1.0 - sg)))
        db = d * (a * sg)
        o_ref[...] = jnp.where(j < nb, da, db).astype(o_ref.dtype)

    return pl.pallas_call(
        body, name=name, grid=(2 * nb, rows // tl),
        in_specs=[pl.BlockSpec((tl, tc), lambda j, i: (i, j % nb)),
                  pl.BlockSpec((tl, tc), lambda j, i: (i, j % nb)),
                  pl.BlockSpec((tl, tc), lambda j, i: (i, j % nb + nb))],
        out_specs=pl.BlockSpec((tl, tc), lambda j, i: (i, j)),
        out_shape=jax.ShapeDtypeStruct((rows, cols2), BF16),
        compiler_params=_params("parallel", "parallel"),
    )(ds, gu, gu)


_GELU_K = math.sqrt(2.0 / math.pi)
_GELU_C = 0.044715


def _gelu(y):
    return 0.5 * y * (1.0 + jnp.tanh(_GELU_K * (y + _GELU_C * y * y * y)))


def _gelu_grad(y):
    t = jnp.tanh(_GELU_K * (y + _GELU_C * y * y * y))
    return 0.5 * (1.0 + t) + 0.5 * y * (1.0 - t * t) * _GELU_K * (1.0 + 3.0 * _GELU_C * y * y)


def _loss_fwd_bwd(y, target, name):
    rows, cols = y.shape

    def fn(j, yb, tb):
        e = yb - tb
        part = 0.5 * jnp.sum(e * e) / cols
        return e * (1.0 / cols), jnp.full((1, LANE), part, F32)

    tl = _row_tile(rows, cols * 16)
    dy, loss = _ew(fn, [y, target], [(cols, F32)], [(LANE, LANE)], rows=rows, cols=cols, tl=tl, name=name)
    return loss, dy


def _softplus(z):
    return jnp.maximum(z, 0.0) + jnp.log(1.0 + jnp.exp(-jnp.abs(z)))


def _split_bf16(v):
    hi = v.astype(BF16)
    lo = (v - hi.astype(F32)).astype(BF16)
    return hi, lo


def _sb_fwd(proj, n_heads, name):
    L = proj.shape[0]
    T = min(SB_TILE, L)
    H = n_heads
    scale = 1.0 / math.sqrt(HEAD_DIM)

    def body(q_ref, k_ref, v_ref, o_ref, s_ref):
        i = pl.program_id(1)
        q = q_ref[...].astype(BF16)
        r0 = lax.broadcasted_iota(jnp.int32, (T, T), 0)
        c0 = lax.broadcasted_iota(jnp.int32, (T, T), 1)
        upper = (r0 > c0).astype(BF16)

        def step(it, carry):
            acc, later = carry
            jb = i - it
            start = pl.multiple_of(jb * T, T)
            kb = k_ref[pl.ds(start, T), :].astype(BF16)
            vb = v_ref[pl.ds(start, T), :].astype(BF16)
            z = lax.dot_general(q, kb, (((1,), (1,)), ((), ())), preferred_element_type=F32) * scale
            before = (jb * T + c0) < (i * T + r0)
            sp = _softplus(z)
            lnot = jnp.where(before, -sp, 0.0)
            hi, lo = _split_bf16(lnot)
            between = (jnp.dot(hi, upper, preferred_element_type=F32)
                       + jnp.dot(lo, upper, preferred_element_type=F32))
            w = jnp.where(before, jnp.exp(z - sp + between + later), 0.0)
            acc = acc + jnp.dot(w.astype(BF16), vb, preferred_element_type=F32)
            later = later + jnp.sum(lnot, axis=1, keepdims=True)
            return acc, later

        acc, later = lax.fori_loop(0, i + 1, step, (jnp.zeros((T, HEAD_DIM), F32), jnp.zeros((T, 1), F32)))
        o_ref[...] = acc
        s_ref[...] = jnp.broadcast_to(later, (T, HEAD_DIM))

    return pl.pallas_call(
        body, name=name, grid=(H, L // T),
        in_specs=[pl.BlockSpec((T, HEAD_DIM), lambda h, i: (i, h)),
                  pl.BlockSpec((L, HEAD_DIM), lambda h, i: (0, H + h)),
                  pl.BlockSpec((L, HEAD_DIM), lambda h, i: (0, 2 * H + h))],
        out_specs=[pl.BlockSpec((T, HEAD_DIM), lambda h, i: (i, h)),
                   pl.BlockSpec((T, HEAD_DIM), lambda h, i: (i, h))],
        out_shape=[jax.ShapeDtypeStruct((L, H * HEAD_DIM), F32), jax.ShapeDtypeStruct((L, H * HEAD_DIM), F32)],
        compiler_params=_params("parallel", "arbitrary"),
    )(proj, proj, proj)


def _sb_bwd(proj, do, rowsum, n_heads, name):
    L = proj.shape[0]
    T = min(SB_TILE, L)
    H = n_heads
    scale = 1.0 / math.sqrt(HEAD_DIM)

    def body(q_ref, k_ref, v_ref, do_ref, s_ref, dq_ref, dk_ref, dv_ref):
        i = pl.program_id(1)

        @pl.when(i == 0)
        def _():
            dk_ref[...] = jnp.zeros_like(dk_ref)
            dv_ref[...] = jnp.zeros_like(dv_ref)

        q = q_ref[...].astype(BF16)
        dob = do_ref[...].astype(BF16)
        total = s_ref[:, 0:1]
        r0 = lax.broadcasted_iota(jnp.int32, (T, T), 0)
        c0 = lax.broadcasted_iota(jnp.int32, (T, T), 1)
        upper = (r0 > c0).astype(BF16)
        lower_inc = (r0 <= c0).astype(BF16)

        def step(jb, carry):
            dq, upto, hsum = carry
            start = pl.multiple_of(jb * T, T)
            kb = k_ref[pl.ds(start, T), :].astype(BF16)
            vb = v_ref[pl.ds(start, T), :].astype(BF16)
            z = lax.dot_general(q, kb, (((1,), (1,)), ((), ())), preferred_element_type=F32) * scale
            before = (jb * T + c0) < (i * T + r0)
            sp = _softplus(z)
            lnot = jnp.where(before, -sp, 0.0)
            upto = upto + jnp.sum(lnot, axis=1, keepdims=True)
            hi, lo = _split_bf16(lnot)
            between = (jnp.dot(hi, upper, preferred_element_type=F32)
                       + jnp.dot(lo, upper, preferred_element_type=F32))
            a = jnp.where(before, jnp.exp(z - sp + between + (total - upto)), 0.0)
            da = lax.dot_general(dob, vb, (((1,), (1,)), ((), ())), preferred_element_type=F32)
            g = a * da
            ghi, glo = _split_bf16(g)
            hcum = (jnp.dot(ghi, lower_inc, preferred_element_type=F32)
                    + jnp.dot(glo, lower_inc, preferred_element_type=F32) + hsum)
            dz = jnp.where(before, g - jnp.exp(z - sp) * hcum, 0.0) * scale
            hsum = hsum + jnp.sum(g, axis=1, keepdims=True)
            dzb = dz.astype(BF16)
            dq = dq + jnp.dot(dzb, kb, preferred_element_type=F32)
            dk_ref[pl.ds(start, T), :] += lax.dot_general(dzb, q, (((0,), (0,)), ((), ())),
                                                          preferred_element_type=F32)
            dv_ref[pl.ds(start, T), :] += lax.dot_general(a.astype(BF16), dob, (((0,), (0,)), ((), ())),
                                                          preferred_element_type=F32)
            return dq, upto, hsum

        init = (jnp.zeros((T, HEAD_DIM), F32), jnp.zeros((T, 1), F32), jnp.zeros((T, 1), F32))
        dq, _, _ = lax.fori_loop(0, i + 1, step, init)
        dq_ref[...] = dq

    shp = jax.ShapeDtypeStruct((L, H * HEAD_DIM), F32)
    return pl.pallas_call(
        body, name=name, grid=(H, L // T),
        in_specs=[pl.BlockSpec((T, HEAD_DIM), lambda h, i: (i, h)),
                  pl.BlockSpec((L, HEAD_DIM), lambda h, i: (0, H + h)),
                  pl.BlockSpec((L, HEAD_DIM), lambda h, i: (0, 2 * H + h)),
                  pl.BlockSpec((T, HEAD_DIM), lambda h, i: (i, h)),
                  pl.BlockSpec((T, HEAD_DIM), lambda h, i: (i, h))],
        out_specs=[pl.BlockSpec((T, HEAD_DIM), lambda h, i: (i, h)),
                   pl.BlockSpec((L, HEAD_DIM), lambda h, i: (0, h)),
                   pl.BlockSpec((L, HEAD_DIM), lambda h, i: (0, h))],
        out_shape=[shp, shp, shp],
        compiler_params=_params("parallel", "arbitrary"),
    )(proj, proj, proj, do, rowsum)


def _head_norm(v):
    r = lax.rsqrt(jnp.mean(v * v, axis=-1, keepdims=True) + EPS)
    return v * r, r


def _memkv_fwd(kv, gk, name):
    M, w2 = kv.shape
    mw = w2 // 2

    def body(kv_ref, gk_ref, k_ref, v_ref):
        for h in range(mw // HEAD_DIM):
            sl = slice(h * HEAD_DIM, (h + 1) * HEAD_DIM)
            xh, _ = _head_norm(kv_ref[:, sl])
            k_ref[:, sl] = xh * gk_ref[...]
        v_ref[...] = kv_ref[:, mw:]

    return pl.pallas_call(body, name=name, out_shape=[jax.ShapeDtypeStruct((M, mw), F32)] * 2,
                          compiler_params=_params())(kv, gk.reshape(1, HEAD_DIM))


def _memkv_bwd(kv, gk, dk, dv, name):
    M, w2 = kv.shape
    mw = w2 // 2

    def body(kv_ref, gk_ref, dk_ref, dv_ref, dkv_ref, dg_ref):
        dg = jnp.zeros((1, HEAD_DIM), F32)
        for h in range(mw // HEAD_DIM):
            sl = slice(h * HEAD_DIM, (h + 1) * HEAD_DIM)
            xh, r = _head_norm(kv_ref[:, sl])
            d = dk_ref[:, sl]
            dg = dg + jnp.sum(d * xh, axis=0, keepdims=True)
            dxh = d * gk_ref[...]
            dkv_ref[:, sl] = r * (dxh - xh * jnp.mean(dxh * xh, axis=-1, keepdims=True))
        dkv_ref[:, mw:] = dv_ref[...]
        dg_ref[...] = dg

    return pl.pallas_call(body, name=name,
                          out_shape=[jax.ShapeDtypeStruct((M, w2), F32), jax.ShapeDtypeStruct((1, HEAD_DIM), F32)],
                          compiler_params=_params())(kv, gk.reshape(1, HEAD_DIM), dk, dv)


def _mem_tile(L):
    return _pick(L, 512, 8)


def _memattn_fwd(proj, khat, v, gq, name):
    L, W = proj.shape
    M, mw = khat.shape
    assert (W - mw) % mw == 0
    qcol = (W - mw) // mw
    tl = _mem_tile(L)
    scale = 1.0 / math.sqrt(HEAD_DIM)

    def body(q_ref, k_ref, v_ref, g_ref, o_ref):
        for h in range(mw // HEAD_DIM):
            sl = slice(h * HEAD_DIM, (h + 1) * HEAD_DIM)
            xh, _ = _head_norm(q_ref[:, sl])
            qh = (xh * g_ref[...]).astype(BF16)
            s = lax.dot_general(qh, k_ref[:, sl].astype(BF16), (((1,), (1,)), ((), ())),
                                preferred_element_type=F32) * scale
            e = jnp.exp(s - jnp.max(s, axis=-1, keepdims=True))
            p = e / jnp.sum(e, axis=-1, keepdims=True)
            o_ref[:, sl] = jnp.dot(p.astype(BF16), v_ref[:, sl].astype(BF16), preferred_element_type=F32)

    whole = lambda a: pl.BlockSpec(a.shape, lambda i: (0,) * a.ndim)
    g2 = gq.reshape(1, HEAD_DIM)
    return pl.pallas_call(
        body, name=name, grid=(L // tl,),
        in_specs=[pl.BlockSpec((tl, mw), lambda i: (i, qcol)), whole(khat), whole(v), whole(g2)],
        out_specs=pl.BlockSpec((tl, mw), lambda i: (i, 0)),
        out_shape=jax.ShapeDtypeStruct((L, mw), F32),
        compiler_params=_params("parallel"),
    )(proj, khat, v, g2)


def _memattn_bwd(proj, khat, v, gq, dcat, name):
    L, W = proj.shape
    M, mw = khat.shape
    qcol = (W - mw) // mw
    dcol = (dcat.shape[1] - mw) // mw
    tl = _mem_tile(L)
    scale = 1.0 / math.sqrt(HEAD_DIM)

    def body(q_ref, k_ref, v_ref, g_ref, do_ref, dq_ref, dk_ref, dv_ref, dg_ref):
        i = pl.program_id(0)

        @pl.when(i == 0)
        def _():
            dk_ref[...] = jnp.zeros_like(dk_ref)
            dv_ref[...] = jnp.zeros_like(dv_ref)
            dg_ref[...] = jnp.zeros_like(dg_ref)

        for h in range(mw // HEAD_DIM):
            sl = slice(h * HEAD_DIM, (h + 1) * HEAD_DIM)
            xh, r = _head_norm(q_ref[:, sl])
            qh = (xh * g_ref[...]).astype(BF16)
            kb = k_ref[:, sl].astype(BF16)
            vb = v_ref[:, sl].astype(BF16)
            dob = do_ref[:, sl].astype(BF16)
            s = lax.dot_general(qh, kb, (((1,), (1,)), ((), ())), preferred_element_type=F32) * scale
            e = jnp.exp(s - jnp.max(s, axis=-1, keepdims=True))
            p = e / jnp.sum(e, axis=-1, keepdims=True)
            dv_ref[:, sl] += lax.dot_general(p.astype(BF16), dob, (((0,), (0,)), ((), ())),
                                             preferred_element_type=F32)
            dp = lax.dot_general(dob, vb, (((1,), (1,)), ((), ())), preferred_element_type=F32)
            ds = (p * (dp - jnp.sum(dp * p, axis=-1, keepdims=True)) * scale).astype(BF16)
            dqh = jnp.dot(ds, kb, preferred_element_type=F32)
            dk_ref[:, sl] += lax.dot_general(ds, qh, (((0,), (0,)), ((), ())), preferred_element_type=F32)
            dg_ref[...] += jnp.sum(dqh * xh, axis=0, keepdims=True)
            dxh = dqh * g_ref[...]
            dq_ref[:, sl] = r * (dxh - xh * jnp.mean(dxh * xh, axis=-1, keepdims=True))

    whole = lambda a: pl.BlockSpec(a.shape, lambda i: (0,) * a.ndim)
    g2 = gq.reshape(1, HEAD_DIM)
    kshape = jax.ShapeDtypeStruct((M, mw), F32)
    return pl.pallas_call(
        body, name=name, grid=(L // tl,),
        in_specs=[pl.BlockSpec((tl, mw), lambda i: (i, qcol)), whole(khat), whole(v), whole(g2),
                  pl.BlockSpec((tl, mw), lambda i: (i, dcol))],
        out_specs=[pl.BlockSpec((tl, mw), lambda i: (i, 0)), whole(kshape), whole(kshape),
                   pl.BlockSpec((1, HEAD_DIM), lambda i: (0, 0))],
        out_shape=[jax.ShapeDtypeStruct((L, mw), F32), kshape, kshape, jax.ShapeDtypeStruct((1, HEAD_DIM), F32)],
        compiler_params=_params("arbitrary"),
    )(proj, khat, v, g2, dcat)


def _cmul(ar, ai, br, bi):
    return ar * br - ai * bi, ar * bi + ai * br


def _s5_discretize(log_dt, a_re, a_im):
    dt = jnp.exp(log_dt)
    mag = jnp.exp(a_re * dt)
    ab_re, ab_im = mag * jnp.cos(a_im * dt), mag * jnp.sin(a_im * dt)
    den = a_re * a_re + a_im * a_im
    inv_re, inv_im = a_re / den, -a_im / den
    f_re, f_im = _cmul(ab_re - 1.0, ab_im, inv_re, inv_im)
    return dt, ab_re, ab_im, inv_re, inv_im, f_re, f_im


def _s5_prep_fwd(log_dt, a_re, a_im, bt_re, bt_im, name):
    G, N = a_re.shape
    C = bt_re.shape[0]

    def body(ld_ref, ar_ref, ai_ref, br_ref, bi_ref, pr_ref, pi_ref, bbr_ref, bbi_ref):
        _, ab_re, ab_im, _, _, f_re, f_im = _s5_discretize(ld_ref[...], ar_ref[...], ai_ref[...])
        p_re, p_im = ab_re, ab_im
        for e in range(8):
            pr_ref[e] = p_re
            pi_ref[e] = p_im
            p_re, p_im = _cmul(p_re, p_im, ab_re, ab_im)
        for ch in range(C):
            bbr_ref[ch], bbi_ref[ch] = _cmul(f_re, f_im, br_ref[ch], bi_ref[ch])

    pw = jax.ShapeDtypeStruct((8, G, N), F32)
    bb = jax.ShapeDtypeStruct((C, G, N), F32)
    return pl.pallas_call(body, name=name, out_shape=[pw, pw, bb, bb], compiler_params=_params())(
        log_dt.reshape(G, 1), a_re, a_im, bt_re, bt_im)


def _s5_prep_bwd(log_dt, a_re, a_im, bt_re, bt_im, ga_re, ga_im, gbb_re, gbb_im, name):
    G, N = a_re.shape
    C = bt_re.shape[0]

    def body(ld_ref, ar_ref, ai_ref, br_ref, bi_ref, gar_ref, gai_ref, gbr_ref, gbi_ref,
             gld_ref, gare_ref, gaim_ref, gbtr_ref, gbti_ref):
        a_re_, a_im_ = ar_ref[...], ai_ref[...]
        dt, ab_re, ab_im, inv_re, inv_im, f_re, f_im = _s5_discretize(ld_ref[...], a_re_, a_im_)
        gf_re, gf_im = jnp.zeros((G, N), F32), jnp.zeros((G, N), F32)
        for ch in range(C):
            gbr, gbi = gbr_ref[ch], gbi_ref[ch]
            gbtr_ref[ch], gbti_ref[ch] = _cmul(f_re, -f_im, gbr, gbi)
            tr, ti = _cmul(br_ref[ch], -bi_ref[ch], gbr, gbi)
            gf_re, gf_im = gf_re + tr, gf_im + ti
        t_re, t_im = _cmul(gf_re, gf_im, inv_re, -inv_im)
        gab_re, gab_im = gar_ref[...] + t_re, gai_ref[...] + t_im
        u_re, u_im = _cmul(dt * ab_re, -dt * ab_im, gab_re, gab_im)
        fl_re, fl_im = _cmul(f_re, f_im, inv_re, inv_im)
        w_re, w_im = _cmul(fl_re, -fl_im, gf_re, gf_im)
        gare_ref[...] = u_re - w_re
        gaim_ref[...] = u_im - w_im
        la_re, la_im = _cmul(a_re_, a_im_, ab_re, ab_im)
        gdt = jnp.sum(la_re * gab_re + la_im * gab_im, axis=1, keepdims=True)
        gld_ref[...] = dt * gdt

    gn = jax.ShapeDtypeStruct((G, N), F32)
    bb = jax.ShapeDtypeStruct((C, G, N), F32)
    return pl.pallas_call(body, name=name,
                          out_shape=[jax.ShapeDtypeStruct((G, 1), F32), gn, gn, bb, bb],
                          compiler_params=_params())(
        log_dt.reshape(G, 1), a_re, a_im, bt_re, bt_im, ga_re, ga_im, gbb_re, gbb_im)


def _dot3(a, b, dims=(((1,), (0,)), ((), ()))):
    ah, al = _split_bf16(a)
    bh, bl = _split_bf16(b)
    d = lambda p, q: lax.dot_general(p, q, dims, preferred_element_type=F32)
    return d(ah, bh) + d(ah, bl) + d(al, bh)


def _block_diag(t):
    nch, P, r, c = t.shape
    eye = jnp.eye(P, dtype=t.dtype)
    return (t[:, :, :, None, :] * eye[None, :, None, :, None]).reshape(nch, P * r, P * c)


def _block_diag_extract(m, P):
    nch, R, Cc = m.shape
    r, c = R // P, Cc // P
    m5 = m.reshape(nch, P, r, P, c)
    return jnp.stack([m5[:, k, :, k, :] for k in range(P)], axis=1)


def _s5_tables(pw_re, pw_im):
    pr = pw_re.reshape(8, -1)
    pi = pw_im.reshape(8, -1)
    row = jnp.arange(8)[:, None]
    tiles = []
    for d in (1, 2, 4):
        keep = row >= d
        tiles += [jnp.where(keep, pr[d - 1][None], 0.0), jnp.where(keep, pi[d - 1][None], 0.0)]
    tiles += [pr, pi]
    for d in (1, 2, 4):
        keep = row <= 7 - d
        tiles += [jnp.where(keep, pr[d - 1][None], 0.0), jnp.where(keep, -pi[d - 1][None], 0.0)]
    tiles += [pr[::-1], -pi[::-1]]
    return jnp.stack(tiles, axis=0)


def _scan_tile(sr, si, m, shifts):
    for n, sh in enumerate(shifts):
        rr, ri = pltpu.roll(sr, sh, 0), pltpu.roll(si, sh, 0)
        mr, mi = m[2 * n], m[2 * n + 1]
        sr, si = sr + mr * rr - mi * ri, si + mr * ri + mi * rr
    return sr, si


def _s5_rows(L):
    return _pick(L, 512, 8)


def _s5_fwd(proj, bd_re, bd_im, cd_re, cd_imn, tables, d_skip, tok_w, name):
    L = proj.shape[0]
    nch = tok_w // LANE
    ns = S5_PACK * S5_STATE
    tl = _s5_rows(L)
    nr = L // tl

    def body(u_ref, bdr_ref, bdi_ref, cdr_ref, cdi_ref, tab_ref, d_ref, y_ref, xr_ref, xi_ref, carry):
        r = pl.program_id(1)

        @pl.when(r == 0)
        def _():
            carry[...] = jnp.zeros_like(carry)

        u = u_ref[...]
        xr_ref[...] = _dot3(u, bdr_ref[0])
        xi_ref[...] = _dot3(u, bdi_ref[0])
        m = [tab_ref[n] for n in range(8)]

        def step(t, c):
            cr, ci = c
            rows = pl.ds(pl.multiple_of(t * 8, 8), 8)
            sr, si = _scan_tile(xr_ref[rows, :], xi_ref[rows, :], m, (1, 2, 4))
            sr, si = sr + m[6] * cr - m[7] * ci, si + m[6] * ci + m[7] * cr
            xr_ref[rows, :] = sr
            xi_ref[rows, :] = si
            return jnp.broadcast_to(sr[7:8, :], (8, ns)), jnp.broadcast_to(si[7:8, :], (8, ns))

        cr, ci = lax.fori_loop(0, tl // 8, step, (carry[0], carry[1]))
        carry[0] = cr
        carry[1] = ci
        y_ref[...] = _dot3(xr_ref[...], cdr_ref[0]) + _dot3(xi_ref[...], cdi_ref[0]) + d_ref[...] * u

    chunk3 = lambda a: pl.BlockSpec((1,) + a.shape[1:], lambda j, r: (j, 0, 0))
    return pl.pallas_call(
        body, name=name, grid=(nch, nr),
        in_specs=[pl.BlockSpec((tl, LANE), lambda j, r: (r, j)),
                  chunk3(bd_re), chunk3(bd_im), chunk3(cd_re), chunk3(cd_imn),
                  pl.BlockSpec((8, 8, ns), lambda j, r: (0, 0, j)),
                  pl.BlockSpec((1, LANE), lambda j, r: (0, j))],
        out_specs=[pl.BlockSpec((tl, LANE), lambda j, r: (r, j)),
                   pl.BlockSpec((tl, ns), lambda j, r: (r, j)),
                   pl.BlockSpec((tl, ns), lambda j, r: (r, j))],
        out_shape=[jax.ShapeDtypeStruct((L, tok_w), F32),
                   jax.ShapeDtypeStruct((L, nch * ns), F32), jax.ShapeDtypeStruct((L, nch * ns), F32)],
        scratch_shapes=[pltpu.VMEM((2, 8, ns), F32)],
        compiler_params=_params("parallel", "arbitrary"),
    )(proj, bd_re, bd_im, cd_re, cd_imn, tables, d_skip.reshape(1, tok_w))


def _s5_bwd(proj, dy, xr, xi, bd_re, bd_im, cd_re, cd_imn, tables, d_skip, tok_w, name):
    L = proj.shape[0]
    nch = tok_w // LANE
    ns = S5_PACK * S5_STATE
    tl = _s5_rows(L)
    nr = L // tl
    tn = (((0,), (0,)), ((), ()))
    nt = (((1,), (1,)), ((), ()))

    def body(u_ref, dy_ref, xr_ref, xi_ref, bdr_ref, bdi_ref, cdr_ref, cdi_ref, tab_ref, d_ref,
             du_ref, gcr_ref, gci_ref, gbr_ref, gbi_ref, gar_ref, gai_ref, gd_ref, lr, li, carry):
        r = pl.program_id(1)

        @pl.when(r == 0)
        def _():
            carry[...] = jnp.zeros_like(carry)
            for ref in (gcr_ref, gci_ref, gbr_ref, gbi_ref, gar_ref, gai_ref, gd_ref):
                ref[...] = jnp.zeros_like(ref)

        u, dyb = u_ref[...], dy_ref[...]
        lr[...] = _dot3(dyb, cdr_ref[0], nt)
        li[...] = _dot3(dyb, cdi_ref[0], nt)
        m = [tab_ref[n] for n in range(8)]
        last = lax.broadcasted_iota(jnp.int32, (8, ns), 0) == 7

        def step(it, c):
            cr, ci, ar, ai = c
            t = tl // 8 - 1 - it
            rows = pl.ds(pl.multiple_of(t * 8, 8), 8)
            sr, si = _scan_tile(lr[rows, :], li[rows, :], m, (7, 6, 4))
            sr, si = sr + m[6] * cr - m[7] * ci, si + m[6] * ci + m[7] * cr
            lr[rows, :] = sr
            li[rows, :] = si
            nr_, ni_ = jnp.where(last, cr, pltpu.roll(sr, 7, 0)), jnp.where(last, ci, pltpu.roll(si, 7, 0))
            xr_t, xi_t = xr_ref[rows, :], xi_ref[rows, :]
            ar = ar + xr_t * nr_ + xi_t * ni_
            ai = ai + xr_t * ni_ - xi_t * nr_
            return jnp.broadcast_to(sr[0:1, :], (8, ns)), jnp.broadcast_to(si[0:1, :], (8, ns)), ar, ai

        z = jnp.zeros((8, ns), F32)
        cr, ci, ar, ai = lax.fori_loop(0, tl // 8, step, (carry[0], carry[1], z, z))
        carry[0] = cr
        carry[1] = ci
        gar_ref[...] += jnp.sum(ar, axis=0, keepdims=True)
        gai_ref[...] += jnp.sum(ai, axis=0, keepdims=True)
        lam_r, lam_i = lr[...], li[...]
        du_ref[...] = _dot3(lam_r, bdr_ref[0], nt) + _dot3(lam_i, bdi_ref[0], nt) + d_ref[...] * dyb
        gd_ref[...] += jnp.sum(dyb * u, axis=0, keepdims=True)
        gcr_ref[0] += _dot3(xr_ref[...], dyb, tn)
        gci_ref[0] += _dot3(xi_ref[...], dyb, tn)
        gbr_ref[0] += _dot3(u, lam_r, tn)
        gbi_ref[0] += _dot3(u, lam_i, tn)

    rev = lambda j, r: (nr - 1 - r, j)
    chunk3 = lambda a: pl.BlockSpec((1,) + a.shape[1:], lambda j, r: (j, 0, 0))
    gcd = jax.ShapeDtypeStruct((nch, ns, LANE), F32)
    gbd = jax.ShapeDtypeStruct((nch, LANE, ns), F32)
    gab = jax.ShapeDtypeStruct((1, nch * ns), F32)
    return pl.pallas_call(
        body, name=name, grid=(nch, nr),
        in_specs=[pl.BlockSpec((tl, LANE), rev), pl.BlockSpec((tl, LANE), rev),
                  pl.BlockSpec((tl, ns), rev), pl.BlockSpec((tl, ns), rev),
                  chunk3(bd_re), chunk3(bd_im), chunk3(cd_re), chunk3(cd_imn),
                  pl.BlockSpec((8, 8, ns), lambda j, r: (1, 0, j)),
                  pl.BlockSpec((1, LANE), lambda j, r: (0, j))],
        out_specs=[pl.BlockSpec((tl, LANE), rev), chunk3(gcd), chunk3(gcd), chunk3(gbd), chunk3(gbd),
                   pl.BlockSpec((1, ns), lambda j, r: (0, j)), pl.BlockSpec((1, ns), lambda j, r: (0, j)),
                   pl.BlockSpec((1, LANE), lambda j, r: (0, j))],
        out_shape=[jax.ShapeDtypeStruct((L, tok_w), F32), gcd, gcd, gbd, gbd, gab, gab,
                   jax.ShapeDtypeStruct((1, tok_w), F32)],
        scratch_shapes=[pltpu.VMEM((tl, ns), F32), pltpu.VMEM((tl, ns), F32), pltpu.VMEM((2, 8, ns), F32)],
        compiler_params=_params("parallel", "arbitrary"),
    )(proj, dy, xr, xi, bd_re, bd_im, cd_re, cd_imn, tables, d_skip.reshape(1, tok_w))


def _lin(dev):
    return 4 * dev[0] + 2 * dev[1] + dev[2]


def _shard_view(ref, axis, index, size):
    ix = tuple(pl.ds(index * size, size) if d == axis else pl.ds(0, ref.shape[d]) for d in range(len(ref.shape)))
    return ref.at[ix]


def _all_gather(shards, axes, name):
    n = len(shards)
    sizes = [s.shape[a] for s, a in zip(shards, axes)]

    def body(*refs):
        ins, outs = refs[:n], refs[n:2 * n]
        send_sems, recv_sems, local_sems = refs[2 * n:]
        x, y, c = lax.axis_index("x"), lax.axis_index("y"), lax.axis_index("c")
        me, sibling = (x, y, c), (x, y, 1 - c)
        chips = [(1 - x, y), (x, 1 - y), (1 - x, 1 - y)]

        def slot(a, dev):
            return _shard_view(outs[a], axes[a], _lin(dev), sizes[a])

        def copy(a, k, block, to, src=None):
            return pltpu.make_async_remote_copy(
                src_ref=slot(a, block) if src is None else src, dst_ref=slot(a, block),
                send_sem=send_sems.at[a, k], recv_sem=recv_sems.at[a, k],
                device_id=to, device_id_type=MESH)

        mine = [pltpu.make_async_copy(ins[a], slot(a, me), local_sems.at[a]) for a in range(n)]
        for cp in mine:
            cp.start()
        first = []
        for a in range(n):
            first.append(copy(a, 0, me, sibling, src=ins[a]))
            first += [copy(a, 1 + j, me, (*chip, c), src=ins[a]) for j, chip in enumerate(chips)]
        for cp in first:
            cp.start()
        passed = []
        for j, chip in enumerate(chips):
            for a in range(n):
                copy(a, 1 + j, (*chip, c), me).wait_recv()
                cp = copy(a, 4 + j, (*chip, c), sibling)
                cp.start()
                passed.append(cp)
        for a in range(n):
            copy(a, 0, sibling, me).wait_recv()
            for j, chip in enumerate(chips):
                copy(a, 4 + j, (*chip, 1 - c), me).wait_recv()
        for cp in first + passed:
            cp.wait_send()
        for cp in mine:
            cp.wait()

    out_shape = []
    for s, a in zip(shards, axes):
        shp = list(s.shape)
        shp[a] *= N_DEV
        out_shape.append(jax.ShapeDtypeStruct(tuple(shp), s.dtype))
    any_spec = pl.BlockSpec(memory_space=pl.ANY)
    return pl.pallas_call(
        body, name=name, in_specs=[any_spec] * n, out_specs=[any_spec] * n, out_shape=out_shape,
        scratch_shapes=[pltpu.SemaphoreType.DMA((n, 7)), pltpu.SemaphoreType.DMA((n, 7)),
                        pltpu.SemaphoreType.DMA((n,))],
    )(*shards)


def _scatter_partials(fulls, axes, name):
    n = len(fulls)
    sizes = [f.shape[a] // N_DEV for f, a in zip(fulls, axes)]
    masks = [(mx, my, mc) for mx in (0, 1) for my in (0, 1) for mc in (0, 1)][1:]

    def body(*refs):
        ins, outs = refs[:n], refs[n:2 * n]
        send_sems, recv_sems, local_sems = refs[2 * n:]
        x, y, c = lax.axis_index("x"), lax.axis_index("y"), lax.axis_index("c")
        me = (x, y, c)
        flip = lambda v, m: 1 - v if m else v
        local = [pltpu.make_async_copy(_shard_view(ins[a], axes[a], _lin(me), sizes[a]),
                                       outs[a].at[_lin(me)], local_sems.at[a]) for a in range(n)]
        for cp in local:
            cp.start()
        sends = []
        for k, (mx, my, mc) in enumerate(masks):
            peer = (flip(x, mx), flip(y, my), flip(c, mc))
            for a in range(n):
                sends.append(pltpu.make_async_remote_copy(
                    src_ref=_shard_view(ins[a], axes[a], _lin(peer), sizes[a]), dst_ref=outs[a].at[_lin(me)],
                    send_sem=send_sems.at[a, k], recv_sem=recv_sems.at[a, k],
                    device_id=peer, device_id_type=MESH))
        for cp in sends:
            cp.start()
        for cp in sends:
            cp.wait()
        for cp in local:
            cp.wait()

    out_shape = []
    for f, a, sz in zip(fulls, axes, sizes):
        shp = list(f.shape)
        shp[a] = sz
        out_shape.append(jax.ShapeDtypeStruct((N_DEV,) + tuple(shp), f.dtype))
    any_spec = pl.BlockSpec(memory_space=pl.ANY)
    return pl.pallas_call(
        body, name=name, in_specs=[any_spec] * n, out_specs=[any_spec] * n, out_shape=out_shape,
        scratch_shapes=[pltpu.SemaphoreType.DMA((n, 7)), pltpu.SemaphoreType.DMA((n, 7)),
                        pltpu.SemaphoreType.DMA((n,))],
    )(*fulls)


def _adamw_sum(w, m, v, parts, name):
    rows, cols = w.shape
    c1 = 1.0 / (1.0 - ADAM_B1 ** ADAM_STEP)
    c2 = 1.0 / (1.0 - ADAM_B2 ** ADAM_STEP)

    def fn(j, wb, mb, vb, *ps):
        g = ps[0].astype(F32)
        for p in ps[1:]:
            g = g + p.astype(F32)
        mn = ADAM_B1 * mb + (1.0 - ADAM_B1) * g
        vn = ADAM_B2 * vb + (1.0 - ADAM_B2) * (g * g)
        delta = -ADAM_LR * ((mn * c1) / (jnp.sqrt(vn * c2) + ADAM_EPS) + ADAM_WD * wb)
        return g, delta, mn, vn

    per_row = cols * (12 + 16 + N_DEV * parts.dtype.itemsize)
    tl = _row_tile(rows, per_row, budget=12 * 1024 * 1024)
    flat = parts.reshape(N_DEV * rows, cols)
    nblk = rows // tl
    ins = [w, m, v] + [(flat, d * nblk, 0) for d in range(N_DEV)]
    return _ew(fn, ins, [(cols, F32)] * 4, rows=rows, cols=cols, tl=tl, name=name)


_NAMES = ['ffn1_norm', 'ffn1_w_gu', 'ffn1_w_down', 'mix_norm', 'mem_norm', 'w_mem_kv', 'xq_norm', 'xk_norm',
          'w_out', 'ffn2_norm', 'ffn2_w_gu', 'ffn2_w_down', 'sb_w_in', 's5_w_in', 's5_log_dt', 's5_a_re',
          's5_a_im', 's5_b_re', 's5_b_im', 's5_c_re', 's5_c_im', 's5_d', 's5_w_glu']
_SHARDED = {'ffn1_w_gu': 1, 'ffn1_w_down': 0, 'w_mem_kv': 0, 'w_out': 0, 'ffn2_w_gu': 1, 'ffn2_w_down': 0,
            'sb_w_in': 1, 's5_w_in': 0, 's5_w_glu': 0}
_PER_LAYER = ['ffn1_w_gu', 'ffn1_w_down', 'w_mem_kv', 'w_out', 'ffn2_w_gu', 'ffn2_w_down']
_SMALL = ['ffn1_norm', 'mix_norm', 'mem_norm', 'xq_norm', 'xk_norm', 'ffn2_norm', 's5_log_dt', 's5_a_re',
          's5_a_im', 's5_b_re', 's5_b_im', 's5_c_re', 's5_c_im']


def _ffn_fwd(x, gain, w_gu, w_down, tag):
    h = _rmsnorm_fwd(x, gain, f"{tag}_norm")
    gu = _mm(h, w_gu, out_dtype=BF16, name=f"{tag}_gu")
    s = _swiglu_fwd(gu, f"{tag}_act")
    out = _mm(s, w_down, res=x, scale=0.5, name=f"{tag}_down")
    return out, (x, h, gu)


def _ffn_bwd(dout, saved, gain, w_gu, w_down, tag):
    x, h, gu = saved
    s = _swiglu_fwd(gu, f"{tag}_act_re")
    g_down = _mm(s, dout, ta=True, scale=0.5, out_dtype=BF16, name=f"{tag}_gdown")
    ds = _mm(dout, w_down, tb=True, scale=0.5, name=f"{tag}_ds")
    dgu = _swiglu_bwd(ds, gu, f"{tag}_dact")
    g_gu = _mm(h, dgu, ta=True, out_dtype=BF16, name=f"{tag}_ggu")
    dh = _mm(dgu, w_gu, tb=True, name=f"{tag}_dh")
    dx, g_gain = _rmsnorm_bwd(dh, x, gain, dout, f"{tag}_dnorm")
    return dx, g_gain, g_gu, g_down


def _mem_fwd(mem, mem_gain, w_kv, gk, tag):
    mem_h = _rmsnorm_fwd(mem, mem_gain, f"{tag}_memnorm")
    kv = _mm(mem_h, w_kv, name=f"{tag}_memkv")
    khat, v = _memkv_fwd(kv, gk, f"{tag}_memk")
    return mem_h, kv, khat, v


def _s5_arrange(p, j, nch):
    bt_re = jnp.transpose(p['s5_b_re'][j], (2, 0, 1))
    bt_im = jnp.transpose(p['s5_b_im'][j], (2, 0, 1))
    return bt_re, bt_im


def _mixer_fwd(i, x, p, W, mem, tag):
    L, D = x.shape
    mw = D // 4
    tok_w = D - mw
    j = i // 2
    h = _rmsnorm_fwd(x, p['mix_norm'][i], f"{tag}_norm")
    mem_h, kv, khat, v = _mem_fwd(mem, p['mem_norm'][i], W['w_mem_kv'], p['xk_norm'][i], tag)
    sv = dict(x=x, h=h, mem_h=mem_h, kv=kv, khat=khat, v=v)
    if i % 2 == 0:
        proj = _mm(h, W['sb_w_in'], name=f"{tag}_in")
        tok, rowsum = _sb_fwd(proj, tok_w // HEAD_DIM, f"{tag}_sb")
        sv.update(rowsum=rowsum)
    else:
        proj = _mm(h, W['s5_w_in'], name=f"{tag}_in")
        nch = tok_w // LANE
        bt_re, bt_im = _s5_arrange(p, j, nch)
        pw_re, pw_im, bb_re, bb_im = _s5_prep_fwd(p['s5_log_dt'][j], p['s5_a_re'][j], p['s5_a_im'][j],
                                                  bt_re, bt_im, f"{tag}_s5prep")
        C, N = bb_re.shape[0], bb_re.shape[2]
        bd_re = _block_diag(jnp.swapaxes(bb_re, 0, 1).reshape(nch, S5_PACK, C, N))
        bd_im = _block_diag(jnp.swapaxes(bb_im, 0, 1).reshape(nch, S5_PACK, C, N))
        ct_re = jnp.swapaxes(p['s5_c_re'][j], 1, 2).reshape(nch, S5_PACK, N, C)
        ct_im = jnp.swapaxes(p['s5_c_im'][j], 1, 2).reshape(nch, S5_PACK, N, C)
        cd_re, cd_imn = _block_diag(ct_re), _block_diag(-ct_im)
        tables = _s5_tables(pw_re, pw_im)
        y, xr, xi = _s5_fwd(proj, bd_re, bd_im, cd_re, cd_imn, tables, W['s5_d'][j], tok_w, f"{tag}_s5")
        tl = _row_tile(L, tok_w * 12)
        yg = _ew(lambda jj, yb: (_gelu(yb),), [y], [(tok_w, BF16)], rows=L, cols=tok_w, tl=tl, name=f"{tag}_gelu")[0]
        pre = _mm(yg, W['s5_w_glu'], name=f"{tag}_glu")
        tok = _ew(lambda jj, yb, pb: (_gelu(yb) * jax.nn.sigmoid(pb),), [y, pre], [(tok_w, F32)],
                  rows=L, cols=tok_w, tl=tl, name=f"{tag}_gate")[0]
        sv.update(y=y, xr=xr, xi=xi, pre=pre, yg=yg, bt_re=bt_re, bt_im=bt_im, bd_re=bd_re, bd_im=bd_im,
                  cd_re=cd_re, cd_imn=cd_imn, tables=tables)
    cross = _memattn_fwd(proj, khat, v, p['xq_norm'][i], f"{tag}_mem")
    cat = jnp.concatenate([tok.astype(BF16), cross.astype(BF16)], axis=1)
    out = _mm(cat, W['w_out'], res=x, name=f"{tag}_out")
    sv.update(proj=proj, cat=cat)
    return out, sv


def _mixer_bwd(i, dout, sv, p, W, mem, tag):
    x, h, proj, cat = sv['x'], sv['h'], sv['proj'], sv['cat']
    L, D = x.shape
    mw = D // 4
    tok_w = D - mw
    j = i // 2
    g = {}
    g['w_out'] = _mm(cat, dout, ta=True, out_dtype=BF16, name=f"{tag}_gout")
    dcat = _mm(dout, W['w_out'], tb=True, name=f"{tag}_dcat")
    dqm, dkhat, dv, g_xq = _memattn_bwd(proj, sv['khat'], sv['v'], p['xq_norm'][i], dcat, f"{tag}_dmem")
    g['xq_norm'] = g_xq
    dkv, g_xk = _memkv_bwd(sv['kv'], p['xk_norm'][i], dkhat, dv, f"{tag}_dmemk")
    g['xk_norm'] = g_xk
    g['w_mem_kv'] = _mm(sv['mem_h'], dkv, ta=True, out_dtype=BF16, name=f"{tag}_gmemkv")
    dmem_h = _mm(dkv, W['w_mem_kv'], tb=True, name=f"{tag}_dmemh")
    _, g['mem_norm'] = _rmsnorm_bwd(dmem_h, mem, p['mem_norm'][i], None, f"{tag}_dmemnorm")
    if i % 2 == 0:
        dq, dk, dvv = _sb_bwd(proj, dcat, sv['rowsum'], tok_w // HEAD_DIM, f"{tag}_dsb")
        dproj = jnp.concatenate([dq.astype(BF16), dk.astype(BF16), dvv.astype(BF16), dqm.astype(BF16)], axis=1)
        w_in = W['sb_w_in']
        g['sb_w_in'] = _mm(h, dproj, ta=True, out_dtype=BF16, name=f"{tag}_gin")
    else:
        y, pre, yg = sv['y'], sv['pre'], sv['yg']
        tl = _row_tile(L, tok_w * 20)

        def gate_bwd(jj, dt, yb, pb):
            gl = _gelu(yb)
            sg = jax.nn.sigmoid(pb)
            return dt * gl * sg * (1.0 - sg)

        dpre = _ew(gate_bwd, [dcat, y, pre], [(tok_w, BF16)], rows=L, cols=tok_w, tl=tl, name=f"{tag}_dgate")[0]
        g['s5_w_glu'] = _mm(yg, dpre, ta=True, out_dtype=BF16, name=f"{tag}_gglu")
        dyg = _mm(dpre, W['s5_w_glu'], tb=True, name=f"{tag}_dyg")

        def gelu_bwd(jj, dt, yb, pb, db):
            return (dt * jax.nn.sigmoid(pb) + db) * _gelu_grad(yb)

        dy = _ew(gelu_bwd, [dcat, y, pre, dyg], [(tok_w, F32)], rows=L, cols=tok_w, tl=tl, name=f"{tag}_dgelu")[0]
        du, gcd_re, gcd_imn, gbd_re, gbd_im, ga_re, ga_im, g_d = _s5_bwd(
            proj, dy, sv['xr'], sv['xi'], sv['bd_re'], sv['bd_im'], sv['cd_re'], sv['cd_imn'], sv['tables'],
            W['s5_d'][j], tok_w, f"{tag}_ds5")
        G, N = p['s5_a_re'].shape[1], p['s5_a_re'].shape[2]
        C = S5_GROUP
        gct_re = _block_diag_extract(gcd_re, S5_PACK).reshape(G, N, C)
        gct_im = -_block_diag_extract(gcd_imn, S5_PACK).reshape(G, N, C)
        g['s5_c_re'] = jnp.swapaxes(gct_re, 1, 2)
        g['s5_c_im'] = jnp.swapaxes(gct_im, 1, 2)
        gbb_re = jnp.swapaxes(_block_diag_extract(gbd_re, S5_PACK).reshape(G, C, N), 0, 1)
        gbb_im = jnp.swapaxes(_block_diag_extract(gbd_im, S5_PACK).reshape(G, C, N), 0, 1)
        g_ld, g_are, g_aim, gbt_re, gbt_im = _s5_prep_bwd(
            p['s5_log_dt'][j], p['s5_a_re'][j], p['s5_a_im'][j], sv['bt_re'], sv['bt_im'],
            ga_re.reshape(G, N), ga_im.reshape(G, N), gbb_re, gbb_im, f"{tag}_ds5prep")
        g['s5_log_dt'] = g_ld.reshape(G)
        g['s5_a_re'], g['s5_a_im'] = g_are, g_aim
        g['s5_b_re'], g['s5_b_im'] = jnp.transpose(gbt_re, (1, 2, 0)), jnp.transpose(gbt_im, (1, 2, 0))
        g['s5_d'] = g_d.reshape(tok_w)
        dproj = jnp.concatenate([du.astype(BF16), dqm.astype(BF16)], axis=1)
        w_in = W['s5_w_in']
        g['s5_w_in'] = _mm(h, dproj, ta=True, out_dtype=BF16, name=f"{tag}_gin")
    dh = _mm(dproj, w_in, tb=True, name=f"{tag}_dh")
    dx, g['mix_norm'] = _rmsnorm_bwd(dh, x, p['mix_norm'][i], dout, f"{tag}_dnorm")
    return dx, g


def _pack(arrs):
    flat = jnp.concatenate([a.reshape(-1).astype(F32) for a in arrs])
    pad = (-flat.shape[0]) % (512 * LANE)
    flat = jnp.pad(flat, (0, pad))
    return flat.reshape(-1, LANE)


def _unpack(flat, like):
    flat = flat.reshape(-1)
    out, off = [], 0
    for a in like:
        n = math.prod(a.shape)
        out.append(flat[off:off + n].reshape(a.shape))
        off += n
    return out


def kernel(x, mem, ffn1_norm, ffn1_w_gu, ffn1_w_down, mix_norm, mem_norm, w_mem_kv, xq_norm, xk_norm, w_out, ffn2_norm, ffn2_w_gu, ffn2_w_down, sb_w_in, s5_w_in, s5_log_dt, s5_a_re, s5_a_im, s5_b_re, s5_b_im, s5_c_re, s5_c_im, s5_d, s5_w_glu, loss_target, m_ffn1_norm, m_ffn1_w_gu, m_ffn1_w_down, m_mix_norm, m_mem_norm, m_w_mem_kv, m_xq_norm, m_xk_norm, m_w_out, m_ffn2_norm, m_ffn2_w_gu, m_ffn2_w_down, m_sb_w_in, m_s5_w_in, m_s5_log_dt, m_s5_a_re, m_s5_a_im, m_s5_b_re, m_s5_b_im, m_s5_c_re, m_s5_c_im, m_s5_d, m_s5_w_glu, v_ffn1_norm, v_ffn1_w_gu, v_ffn1_w_down, v_mix_norm, v_mem_norm, v_w_mem_kv, v_xq_norm, v_xk_norm, v_w_out, v_ffn2_norm, v_ffn2_w_gu, v_ffn2_w_down, v_sb_w_in, v_s5_w_in, v_s5_log_dt, v_s5_a_re, v_s5_a_im, v_s5_b_re, v_s5_b_im, v_s5_c_re, v_s5_c_im, v_s5_d, v_s5_w_glu):
    given = dict(locals())
    p = {n: given[n] for n in _NAMES}
    mom = {n: given["m_" + n] for n in _NAMES}
    var = {n: given["v_" + n] for n in _NAMES}
    depth = ffn1_norm.shape[0]
    xs, mems, target = x[0], mem[0], loss_target[0]
    L, D = xs.shape
    me = 4 * lax.axis_index("x") + 2 * lax.axis_index("y") + lax.axis_index("c")

    def layer_slice(name, i):
        return p[name][i // 2] if name in ('sb_w_in', 's5_w_in', 's5_w_glu') else p[name][i]

    def layer_names(i):
        return _PER_LAYER + (['sb_w_in'] if i % 2 == 0 else ['s5_w_in', 's5_w_glu'])

    n_b = s5_d.shape[0]
    d_rows = jnp.broadcast_to(s5_d.reshape(1, -1), (8, s5_d.size))
    d_full = _all_gather([d_rows], [0], "gather_s5d")[0][::8]
    d_full = jnp.swapaxes(d_full.reshape(N_DEV, n_b, -1), 0, 1).reshape(n_b, -1)

    Ws, saves = [], []
    act = xs
    for i in range(depth):
        names = layer_names(i)
        shards = [_cast_bf16(layer_slice(nm, i), f"cast_{nm}") for nm in names]
        full = _all_gather(shards, [_SHARDED[nm] for nm in names], f"gather_w{i % 2}")
        W = dict(zip(names, full))
        W['s5_d'] = d_full
        Ws.append(W)
        act, s1 = _ffn_fwd(act, p['ffn1_norm'][i], W['ffn1_w_gu'], W['ffn1_w_down'], "ffn1")
        act, s2 = _mixer_fwd(i, act, p, W, mems, f"mix{i % 2}")
        act, s3 = _ffn_fwd(act, p['ffn2_norm'][i], W['ffn2_w_gu'], W['ffn2_w_down'], "ffn2")
        saves.append((s1, s2, s3))

    loss_part, dact = _loss_fwd_bwd(act, target, "loss")
    loss = lax.psum(loss_part[0, 0], ("x", "y", "c"))

    small_g = {n: [None] * p[n].shape[0] for n in _SMALL + ['s5_d']}
    big = {n: [None] * p[n].shape[0] for n in _SHARDED}
    for i in reversed(range(depth)):
        W = Ws[i]
        s1, s2, s3 = saves[i]
        j = i // 2
        dact, gg, g_gu2, g_dn2 = _ffn_bwd(dact, s3, p['ffn2_norm'][i], W['ffn2_w_gu'], W['ffn2_w_down'], "ffn2")
        small_g['ffn2_norm'][i] = gg
        dact, gm = _mixer_bwd(i, dact, s2, p, W, mems, f"mix{i % 2}")
        dact, gg, g_gu1, g_dn1 = _ffn_bwd(dact, s1, p['ffn1_norm'][i], W['ffn1_w_gu'], W['ffn1_w_down'], "ffn1")
        small_g['ffn1_norm'][i] = gg
        gm.update(ffn1_w_gu=g_gu1, ffn1_w_down=g_dn1, ffn2_w_gu=g_gu2, ffn2_w_down=g_dn2)
        for n in _SMALL + ['s5_d']:
            if n in gm:
                small_g[n][j if n.startswith('s5_') else i] = gm[n]
        names = layer_names(i)
        parts = _scatter_partials([gm[nm] for nm in names], [_SHARDED[nm] for nm in names], f"scatter_g{i % 2}")
        for nm, part in zip(names, parts):
            li = j if nm in ('sb_w_in', 's5_w_in', 's5_w_glu') else i
            big[nm][li] = _adamw_sum(p[nm][li], mom[nm][li], var[nm][li], part, f"adamw_{nm}")

    small_names = _SMALL + ['s5_d']
    g_small = [jnp.stack([gi.reshape(p[n].shape[1:]) if n != 's5_d' else gi for gi in small_g[n]])
               for n in small_names]
    d_w = jnp.zeros_like(d_full)
    cw = s5_d.shape[1]
    w_small = [p[n] for n in _SMALL]
    m_small = [mom[n] for n in _SMALL]
    v_small = [var[n] for n in _SMALL]
    place = lambda a: lax.dynamic_update_slice(d_w, a, (0, me * cw))
    flat_g = _pack(g_small)
    flat_w = _pack(w_small + [place(s5_d)])
    flat_m = _pack(m_small + [place(mom['s5_d'])])
    flat_v = _pack(v_small + [place(var['s5_d'])])
    parts = _all_gather([flat_g.reshape((1,) + flat_g.shape)], [0], "gather_small")[0]
    res_small = _adamw_sum(flat_w, flat_m, flat_v, parts, "adamw_small")
    like = [p[n] for n in _SMALL] + [d_full]
    un = [_unpack(r, like) for r in res_small]

    def result(kind, n):
        if n in _SHARDED:
            return jnp.stack([r[kind] for r in big[n]])
        if n == 's5_d':
            return lax.dynamic_slice(un[kind][-1], (0, me * cw), (n_b, cw))
        return un[kind][_SMALL.index(n)]

    outs = [loss, dact.reshape((1,) + dact.shape)]
    for kind in range(4):
        outs += [result(kind, n) for n in _NAMES]
    return tuple(outs)
```

```python
import functools
import math

import jax
import jax.numpy as jnp
from jax import lax
from jax.experimental import pallas as pl
from jax.experimental.pallas import tpu as pltpu

F32 = jnp.float32
BF16 = jnp.bfloat16
MESH = pl.DeviceIdType.MESH

HEAD_DIM = 128
S5_GROUP = 16
S5_STATE = 64
S5_PACK = 8
EPS = 1e-6
ADAM_LR, ADAM_B1, ADAM_B2, ADAM_EPS, ADAM_WD, ADAM_STEP = 0.001, 0.9, 0.999, 1e-08, 0.01, 10
N_DEV = 8
VMEM_LIMIT_BYTES = 56 * 1024 * 1024
SB_TILE = 256
LANE = 128


def _params(*sem):
    if sem:
        return pltpu.CompilerParams(dimension_semantics=sem, vmem_limit_bytes=VMEM_LIMIT_BYTES)
    return pltpu.CompilerParams(vmem_limit_bytes=VMEM_LIMIT_BYTES)


def _pick(dim, target, align=LANE):
    if dim <= target:
        return dim
    best = None
    for t in range(align, target + 1, align):
        if dim % t == 0:
            best = t
    assert best is not None, (dim, target, align)
    return best


MM_OPERAND_BYTES = 20 * 1024 * 1024
ANY_SPEC = pl.BlockSpec(memory_space=pl.ANY)


class _Side:
    def __init__(self, ins, out_shape, scratch, start, finish):
        self.ins, self.out_shape, self.scratch, self.start, self.finish = ins, out_shape, scratch, start, finish


def _with_side(side, n_in, n_out, n_scratch, refs, first, last, compute):
    if side is None:
        compute(refs)
        return
    ns_in, ns_out = len(side.ins), len(side.out_shape)
    own_in, s_in = refs[:n_in], refs[n_in:n_in + ns_in]
    o0 = n_in + ns_in
    own_out, s_out = refs[o0:o0 + n_out], refs[o0 + n_out:o0 + n_out + ns_out]
    c0 = o0 + n_out + ns_out
    own_scr, s_scr = refs[c0:c0 + n_scratch], refs[c0 + n_scratch:]

    @pl.when(first)
    def _():
        side.start(s_in, s_out, s_scr)

    compute(tuple(own_in) + tuple(own_out) + tuple(own_scr))

    @pl.when(last)
    def _():
        side.finish(s_in, s_out, s_scr)


def _side_call(body_own, side, *, name, grid, in_specs, out_specs, out_shape, scratch, args, sem):
    n_in, n_out, n_scr = len(args), len(out_shape), len(scratch)
    if side is not None:
        in_specs = list(in_specs) + [ANY_SPEC] * len(side.ins)
        out_specs = list(out_specs) + [ANY_SPEC] * len(side.out_shape)
        out_shape = list(out_shape) + list(side.out_shape)
        scratch = list(scratch) + list(side.scratch)
        args = list(args) + list(side.ins)
        sem = ("arbitrary",) * len(grid)

    def body(*refs):
        ids = [pl.program_id(d) for d in range(len(grid))]
        first, last = ids[0] == 0, ids[0] == grid[0] - 1
        for d in range(1, len(grid)):
            first = jnp.logical_and(first, ids[d] == 0)
            last = jnp.logical_and(last, ids[d] == grid[d] - 1)
        _with_side(side, n_in, n_out, n_scr, refs, first, last, body_own)

    res = pl.pallas_call(body, name=name, grid=grid, in_specs=in_specs, out_specs=out_specs,
                         out_shape=out_shape, scratch_shapes=scratch, compiler_params=_params(*sem))(*args)
    return res[:n_out], res[n_out:]


def _run_side(side, name):
    def body(*refs):
        ni, no = len(side.ins), len(side.out_shape)
        side.start(refs[:ni], refs[ni:ni + no], refs[ni + no:])
        side.finish(refs[:ni], refs[ni:ni + no], refs[ni + no:])

    return pl.pallas_call(body, name=name, in_specs=[ANY_SPEC] * len(side.ins),
                          out_specs=[ANY_SPEC] * len(side.out_shape), out_shape=list(side.out_shape),
                          scratch_shapes=list(side.scratch))(*side.ins)


def _mm(a, b, *, ta=False, tb=False, out_dtype=F32, res=None, scale=None, name, side=None):
    if ta:
        K, M = a.shape
    else:
        M, K = a.shape
    if tb:
        N, Kb = b.shape
    else:
        Kb, N = b.shape
    assert K == Kb, (a.shape, b.shape, ta, tb)
    tm, tn = _pick(M, 1024), _pick(N, 1024)
    per_k = 2 * (tm * a.dtype.itemsize + tn * b.dtype.itemsize)
    tk = _pick(K, max(LANE, MM_OPERAND_BYTES // per_k))
    nk = K // tk
    dims = (((0 if ta else 1,), (1 if tb else 0,)), ((), ()))

    def finish(o, r_ref, o_ref):
        if scale is not None:
            o = o * scale
        if r_ref is not None:
            o = r_ref[...].astype(F32) + o
        o_ref[...] = o.astype(o_ref.dtype)

    def compute(refs):
        a_ref, b_ref = refs[0], refs[1]
        r_ref = refs[2] if res is not None else None
        o_ref = refs[3] if res is not None else refs[2]
        part = lax.dot_general(a_ref[...].astype(BF16), b_ref[...].astype(BF16), dims, preferred_element_type=F32)
        if nk == 1:
            finish(part, r_ref, o_ref)
            return
        acc = refs[-1]
        k = pl.program_id(2)

        @pl.when(k == 0)
        def _():
            acc[...] = part

        @pl.when(k > 0)
        def _():
            acc[...] += part

        @pl.when(k == nk - 1)
        def _():
            finish(acc[...], r_ref, o_ref)

    a_spec = pl.BlockSpec((tk, tm), lambda i, j, k: (k, i)) if ta else pl.BlockSpec((tm, tk), lambda i, j, k: (i, k))
    b_spec = pl.BlockSpec((tn, tk), lambda i, j, k: (j, k)) if tb else pl.BlockSpec((tk, tn), lambda i, j, k: (k, j))
    o_spec = pl.BlockSpec((tm, tn), lambda i, j, k: (i, j))
    in_specs, args = [a_spec, b_spec], [a, b]
    if res is not None:
        in_specs.append(o_spec)
        args.append(res)
    own, extra = _side_call(
        compute, side, name=name, grid=(M // tm, N // tn, nk), in_specs=in_specs, out_specs=[o_spec],
        out_shape=[jax.ShapeDtypeStruct((M, N), out_dtype)],
        scratch=[pltpu.VMEM((tm, tn), F32)] if nk > 1 else [], args=args,
        sem=("parallel", "parallel", "arbitrary"))
    return own[0] if side is None else (own[0], extra)


def _silu_grads(g):
    sg = jax.nn.sigmoid(g)
    return g * sg, sg * (1.0 + g * (1.0 - sg))


def _mm_swiglu(h, w_gu, name, side=None):
    M, K = h.shape
    n2 = w_gu.shape[1]
    c = n2 // N_DEV
    tm = _pick(M, 512)

    def compute(refs):
        h_ref, w_ref, gu_ref, s_ref = refs
        r = jnp.dot(h_ref[...], w_ref[...], preferred_element_type=F32)
        gu_ref[...] = r.astype(gu_ref.dtype)
        act, _ = _silu_grads(r[:, :c])
        s_ref[...] = (act * r[:, c:]).astype(s_ref.dtype)

    own, extra = _side_call(
        compute, side, name=name, grid=(N_DEV // 2, M // tm),
        in_specs=[pl.BlockSpec((tm, K), lambda j, i: (i, 0)), pl.BlockSpec((K, 2 * c), lambda j, i: (0, j))],
        out_specs=[pl.BlockSpec((tm, 2 * c), lambda j, i: (i, j)), pl.BlockSpec((tm, c), lambda j, i: (i, j))],
        out_shape=[jax.ShapeDtypeStruct((M, n2), BF16), jax.ShapeDtypeStruct((M, n2 // 2), BF16)],
        scratch=[], args=[h, w_gu], sem=("parallel", "parallel"))
    return own[0], own[1], extra


def _mm_dswiglu(dout, w_down, gu, name, side=None):
    M, D = dout.shape
    F_ = w_down.shape[0]
    c = F_ // (N_DEV // 2)
    tm = _pick(M, 512)
    nt = (((1,), (1,)), ((), ()))

    def compute(refs):
        d_ref, w_ref, gu_ref, o_ref = refs
        ds = 0.5 * lax.dot_general(d_ref[...], w_ref[...], nt, preferred_element_type=F32)
        g, u = gu_ref[:, :c].astype(F32), gu_ref[:, c:].astype(F32)
        act, dact = _silu_grads(g)
        o_ref[:, :c] = (ds * u * dact).astype(o_ref.dtype)
        o_ref[:, c:] = (ds * act).astype(o_ref.dtype)

    own, extra = _side_call(
        compute, side, name=name, grid=(N_DEV // 2, M // tm),
        in_specs=[pl.BlockSpec((tm, D), lambda j, i: (i, 0)), pl.BlockSpec((c, D), lambda j, i: (j, 0)),
                  pl.BlockSpec((tm, 2 * c), lambda j, i: (i, j))],
        out_specs=[pl.BlockSpec((tm, 2 * c), lambda j, i: (i, j))],
        out_shape=[jax.ShapeDtypeStruct((M, 2 * F_), BF16)],
        scratch=[], args=[dout, w_down, gu], sem=("parallel", "parallel"))
    return own[0], extra


class _Whole:
    def __init__(self, a):
        self.a = a


def _ew(fn, ins, outs, accs=(), *, rows, cols, tl, tc=None, name):
    tc = cols if tc is None else tc
    nj, ni = cols // tc, rows // tl
    assert nj * tc == cols and ni * tl == rows, (rows, cols, tl, tc)
    in_specs, args = [], []
    for it in ins:
        if isinstance(it, _Whole):
            in_specs.append(pl.BlockSpec(it.a.shape, lambda j, i, _n=it.a.ndim: (0,) * _n))
            args.append(it.a)
        else:
            arr, ro, co = it if isinstance(it, tuple) else (it, 0, 0)
            in_specs.append(pl.BlockSpec((tl, tc), lambda j, i, _ro=ro, _co=co: (i + _ro, j + _co)))
            args.append(arr)
    out_shape = [jax.ShapeDtypeStruct((rows, c), dt) for c, dt in outs]
    out_specs = [pl.BlockSpec((tl, tc), lambda j, i: (i, j)) for _ in outs]
    for c, t in accs:
        out_shape.append(jax.ShapeDtypeStruct((1, c), F32))
        out_specs.append(pl.BlockSpec((1, t), lambda j, i: (0, j)))
    n_in, n_out = len(args), len(outs)

    def body(*refs):
        j, i = pl.program_id(0), pl.program_id(1)
        vals = fn(j, *[r[...] for r in refs[:n_in]])
        if not isinstance(vals, (tuple, list)):
            vals = (vals,)
        for r, v in zip(refs[n_in:n_in + n_out], vals[:n_out]):
            r[...] = v.astype(r.dtype)
        for r, v in zip(refs[n_in + n_out:], vals[n_out:]):
            @pl.when(i == 0)
            def _(r=r):
                r[...] = jnp.zeros_like(r)
            r[...] += v

    res = pl.pallas_call(
        body, name=name, grid=(nj, ni), in_specs=in_specs, out_specs=out_specs, out_shape=out_shape,
        compiler_params=_params("parallel", "arbitrary"),
    )(*args)
    return res


def _row_tile(rows, bytes_per_row, budget=10 * 1024 * 1024, align=16):
    cap = max(align, budget // max(1, bytes_per_row))
    best = None
    for t in range(align, min(rows, cap) + 1, align):
        if rows % t == 0:
            best = t
    if best is None:
        best = rows
    return best


def _cast_bf16(w2d, name):
    rows, cols = w2d.shape
    tl = _row_tile(rows, cols * 6)
    return _ew(lambda j, w: w, [w2d], [(cols, BF16)], rows=rows, cols=cols, tl=tl, name=name)[0]


def _rmsnorm_fwd(x, g, name, out_dtype=BF16):
    rows, cols = x.shape

    def fn(j, xb, gb):
        r = lax.rsqrt(jnp.mean(xb * xb, axis=-1, keepdims=True) + EPS)
        return (xb * r * gb,)

    tl = _row_tile(rows, cols * 12)
    return _ew(fn, [x, _Whole(g.reshape(1, cols))], [(cols, out_dtype)], rows=rows, cols=cols, tl=tl, name=name)[0]


def _rmsnorm_bwd(dh, x, g, dres, name):
    rows, cols = x.shape

    def fn(j, dhb, xb, gb, *rest):
        r = lax.rsqrt(jnp.mean(xb * xb, axis=-1, keepdims=True) + EPS)
        xh = xb * r
        dxh = dhb * gb
        dx = r * (dxh - xh * jnp.mean(dxh * xh, axis=-1, keepdims=True))
        if rest:
            dx = dx + rest[0]
        return dx, dx, jnp.sum(dhb * xh, axis=0, keepdims=True)

    ins = [dh, x, _Whole(g.reshape(1, cols))] + ([dres] if dres is not None else [])
    tl = _row_tile(rows, cols * 28)
    return _ew(fn, ins, [(cols, F32), (cols, BF16)], [(cols, cols)], rows=rows, cols=cols, tl=tl, name=name)


_GELU_K = math.sqrt(2.0 / math.pi)
_GELU_C = 0.044715


def _gelu(y):
    return 0.5 * y * (1.0 + jnp.tanh(_GELU_K * (y + _GELU_C * y * y * y)))


def _gelu_grad(y):
    t = jnp.tanh(_GELU_K * (y + _GELU_C * y * y * y))
    return 0.5 * (1.0 + t) + 0.5 * y * (1.0 - t * t) * _GELU_K * (1.0 + 3.0 * _GELU_C * y * y)


def _loss_fwd_bwd(y, target, name):
    rows, cols = y.shape

    def fn(j, yb, tb):
        e = yb - tb
        part = 0.5 * jnp.sum(e * e) / cols
        dy = e * (1.0 / cols)
        return dy, dy, jnp.full((1, LANE), part, F32)

    tl = _row_tile(rows, cols * 20)
    dy, dy16, loss = _ew(fn, [y, target], [(cols, F32), (cols, BF16)], [(LANE, LANE)], rows=rows, cols=cols,
                         tl=tl, name=name)
    return loss, dy, dy16


def _sb_terms(z):
    l1p = jnp.log(1.0 + jnp.exp(-jnp.abs(z)))
    return jnp.maximum(z, 0.0) + l1p, jnp.minimum(z, 0.0) - l1p


def _split_bf16(v):
    hi = v.astype(BF16)
    lo = (v - hi.astype(F32)).astype(BF16)
    return hi, lo


def _sb_fwd(proj, n_heads, name):
    L = proj.shape[0]
    T = min(SB_TILE, L // 2)
    H = n_heads
    scale = 1.0 / math.sqrt(HEAD_DIM)
    nt = (((1,), (1,)), ((), ()))

    def body(q_ref, k_ref, v_ref, o_ref, s_ref):
        i = pl.program_id(1)
        q_lo, q_hi = q_ref[0:T, :].astype(BF16), q_ref[T:2 * T, :].astype(BF16)
        r0 = lax.broadcasted_iota(jnp.int32, (T, T), 0)
        c0 = lax.broadcasted_iota(jnp.int32, (T, T), 1)
        strict = r0 > c0
        upper = strict.astype(BF16)

        def keys(jb):
            start = pl.multiple_of(jb * T, T)
            return k_ref[pl.ds(start, T), :].astype(BF16), v_ref[pl.ds(start, T), :].astype(BF16)

        def tile(q, kb, vb, carry, diagonal):
            acc, later = carry
            z = lax.dot_general(q, kb, nt, preferred_element_type=F32) * scale
            sp, logsig = _sb_terms(z)
            lnot = jnp.where(strict, -sp, 0.0) if diagonal else -sp
            hi, lo = _split_bf16(lnot)
            between = (jnp.dot(hi, upper, preferred_element_type=F32)
                       + jnp.dot(lo, upper, preferred_element_type=F32))
            w = jnp.exp(logsig + between + later)
            if diagonal:
                w = jnp.where(strict, w, 0.0)
            acc = acc + jnp.dot(w.astype(BF16), vb, preferred_element_type=F32)
            return acc, later + jnp.sum(lnot, axis=1, keepdims=True)

        zero = (jnp.zeros((T, HEAD_DIM), F32), jnp.zeros((T, 1), F32))
        kb, vb = keys(2 * i + 1)
        hi_c = tile(q_hi, kb, vb, zero, True)
        kb, vb = keys(2 * i)
        lo_c = tile(q_lo, kb, vb, zero, True)
        hi_c = tile(q_hi, kb, vb, hi_c, False)

        def step(it, carry):
            kb, vb = keys(2 * i - 1 - it)
            return tile(q_lo, kb, vb, carry[0], False), tile(q_hi, kb, vb, carry[1], False)

        lo_c, hi_c = lax.fori_loop(0, 2 * i, step, (lo_c, hi_c))
        o_ref[0:T, :] = lo_c[0]
        o_ref[T:2 * T, :] = hi_c[0]
        s_ref[0:T, :] = jnp.broadcast_to(lo_c[1], (T, HEAD_DIM))
        s_ref[T:2 * T, :] = jnp.broadcast_to(hi_c[1], (T, HEAD_DIM))

    return pl.pallas_call(
        body, name=name, grid=(H, L // (2 * T)),
        in_specs=[pl.BlockSpec((2 * T, HEAD_DIM), lambda h, i: (i, h)),
                  pl.BlockSpec((L, HEAD_DIM), lambda h, i: (0, H + h)),
                  pl.BlockSpec((L, HEAD_DIM), lambda h, i: (0, 2 * H + h))],
        out_specs=[pl.BlockSpec((2 * T, HEAD_DIM), lambda h, i: (i, h)),
                   pl.BlockSpec((2 * T, HEAD_DIM), lambda h, i: (i, h))],
        out_shape=[jax.ShapeDtypeStruct((L, H * HEAD_DIM), F32), jax.ShapeDtypeStruct((L, H * HEAD_DIM), F32)],
        compiler_params=_params("parallel", "arbitrary"),
    )(proj, proj, proj)


def _sb_bwd(proj, do, rowsum, n_heads, name):
    L = proj.shape[0]
    T = min(SB_TILE, L // 2)
    H = n_heads
    scale = 1.0 / math.sqrt(HEAD_DIM)
    nt = (((1,), (1,)), ((), ()))
    tn = (((0,), (0,)), ((), ()))

    def body(q_ref, k_ref, v_ref, do_ref, s_ref, dq_ref, dk_ref, dv_ref):
        i = pl.program_id(1)

        @pl.when(i == 0)
        def _():
            dk_ref[...] = jnp.zeros_like(dk_ref)
            dv_ref[...] = jnp.zeros_like(dv_ref)

        q_all, do_all = q_ref[...].astype(BF16), do_ref[...].astype(BF16)
        q_lo, q_hi = q_all[0:T], q_all[T:2 * T]
        do_lo, do_hi = do_all[0:T], do_all[T:2 * T]
        tot_lo, tot_hi = s_ref[0:T, 0:1], s_ref[T:2 * T, 0:1]
        r0 = lax.broadcasted_iota(jnp.int32, (T, T), 0)
        c0 = lax.broadcasted_iota(jnp.int32, (T, T), 1)
        strict = r0 > c0
        upper = strict.astype(BF16)
        lower_inc = (r0 <= c0).astype(BF16)

        def keys(jb):
            start = pl.multiple_of(jb * T, T)
            return start, k_ref[pl.ds(start, T), :].astype(BF16), v_ref[pl.ds(start, T), :].astype(BF16)

        def tile(q, dob, total, kb, vb, carry, diagonal):
            dq, upto, hsum = carry
            z = lax.dot_general(q, kb, nt, preferred_element_type=F32) * scale
            sp, logsig = _sb_terms(z)
            lnot = jnp.where(strict, -sp, 0.0) if diagonal else -sp
            upto = upto + jnp.sum(lnot, axis=1, keepdims=True)
            hi, lo = _split_bf16(lnot)
            between = (jnp.dot(hi, upper, preferred_element_type=F32)
                       + jnp.dot(lo, upper, preferred_element_type=F32))
            a = jnp.exp(logsig + between + (total - upto))
            if diagonal:
                a = jnp.where(strict, a, 0.0)
            g = a * lax.dot_general(dob, vb, nt, preferred_element_type=F32)
            ghi, glo = _split_bf16(g)
            hcum = (jnp.dot(ghi, lower_inc, preferred_element_type=F32)
                    + jnp.dot(glo, lower_inc, preferred_element_type=F32) + hsum)
            dz = (g - jnp.exp(logsig) * hcum) * scale
            if diagonal:
                dz = jnp.where(strict, dz, 0.0)
            dzb = dz.astype(BF16)
            dq = dq + jnp.dot(dzb, kb, preferred_element_type=F32)
            return (dq, upto, hsum + jnp.sum(g, axis=1, keepdims=True)), dzb, a.astype(BF16)

        def both(start, dz_lo, a_lo, dz_hi, a_hi):
            dz2, a2 = jnp.concatenate([dz_lo, dz_hi], axis=0), jnp.concatenate([a_lo, a_hi], axis=0)
            dk_ref[pl.ds(start, T), :] += lax.dot_general(dz2, q_all, tn, preferred_element_type=F32)
            dv_ref[pl.ds(start, T), :] += lax.dot_general(a2, do_all, tn, preferred_element_type=F32)

        def step(jb, carry):
            start, kb, vb = keys(jb)
            c_lo, dz_lo, a_lo = tile(q_lo, do_lo, tot_lo, kb, vb, carry[0], False)
            c_hi, dz_hi, a_hi = tile(q_hi, do_hi, tot_hi, kb, vb, carry[1], False)
            both(start, dz_lo, a_lo, dz_hi, a_hi)
            return c_lo, c_hi

        zero = (jnp.zeros((T, HEAD_DIM), F32), jnp.zeros((T, 1), F32), jnp.zeros((T, 1), F32))
        c_lo, c_hi = lax.fori_loop(0, 2 * i, step, (zero, zero))
        start, kb, vb = keys(2 * i)
        c_lo, dz_lo, a_lo = tile(q_lo, do_lo, tot_lo, kb, vb, c_lo, True)
        c_hi, dz_hi, a_hi = tile(q_hi, do_hi, tot_hi, kb, vb, c_hi, False)
        both(start, dz_lo, a_lo, dz_hi, a_hi)
        start, kb, vb = keys(2 * i + 1)
        c_hi, dz_hi, a_hi = tile(q_hi, do_hi, tot_hi, kb, vb, c_hi, True)
        dk_ref[pl.ds(start, T), :] += lax.dot_general(dz_hi, q_hi, tn, preferred_element_type=F32)
        dv_ref[pl.ds(start, T), :] += lax.dot_general(a_hi, do_hi, tn, preferred_element_type=F32)
        dq_ref[0:T, :] = c_lo[0]
        dq_ref[T:2 * T, :] = c_hi[0]

    shp = jax.ShapeDtypeStruct((L, H * HEAD_DIM), F32)
    return pl.pallas_call(
        body, name=name, grid=(H, L // (2 * T)),
        in_specs=[pl.BlockSpec((2 * T, HEAD_DIM), lambda h, i: (i, h)),
                  pl.BlockSpec((L, HEAD_DIM), lambda h, i: (0, H + h)),
                  pl.BlockSpec((L, HEAD_DIM), lambda h, i: (0, 2 * H + h)),
                  pl.BlockSpec((2 * T, HEAD_DIM), lambda h, i: (i, h)),
                  pl.BlockSpec((2 * T, HEAD_DIM), lambda h, i: (i, h))],
        out_specs=[pl.BlockSpec((2 * T, HEAD_DIM), lambda h, i: (i, h)),
                   pl.BlockSpec((L, HEAD_DIM), lambda h, i: (0, h)),
                   pl.BlockSpec((L, HEAD_DIM), lambda h, i: (0, h))],
        out_shape=[shp, shp, shp],
        compiler_params=_params("parallel", "arbitrary"),
    )(proj, proj, proj, do, rowsum)


def _head_norm(v):
    r = lax.rsqrt(jnp.mean(v * v, axis=-1, keepdims=True) + EPS)
    return v * r, r


def _memkv_fwd(kv, gk, name):
    M, w2 = kv.shape
    mw = w2 // 2

    def body(kv_ref, gk_ref, k_ref, v_ref):
        for h in range(mw // HEAD_DIM):
            sl = slice(h * HEAD_DIM, (h + 1) * HEAD_DIM)
            xh, _ = _head_norm(kv_ref[:, sl])
            k_ref[:, sl] = xh * gk_ref[...]
        v_ref[...] = kv_ref[:, mw:]

    return pl.pallas_call(body, name=name, out_shape=[jax.ShapeDtypeStruct((M, mw), F32)] * 2,
                          compiler_params=_params())(kv, gk.reshape(1, HEAD_DIM))


def _memkv_bwd(kv, gk, dk, dv, name):
    M, w2 = kv.shape
    mw = w2 // 2

    def body(kv_ref, gk_ref, dk_ref, dv_ref, dkv_ref, dg_ref):
        dg = jnp.zeros((1, HEAD_DIM), F32)
        for h in range(mw // HEAD_DIM):
            sl = slice(h * HEAD_DIM, (h + 1) * HEAD_DIM)
            xh, r = _head_norm(kv_ref[:, sl])
            d = dk_ref[:, sl]
            dg = dg + jnp.sum(d * xh, axis=0, keepdims=True)
            dxh = d * gk_ref[...]
            dkv_ref[:, sl] = r * (dxh - xh * jnp.mean(dxh * xh, axis=-1, keepdims=True))
        dkv_ref[:, mw:] = dv_ref[...]
        dg_ref[...] = dg

    return pl.pallas_call(body, name=name,
                          out_shape=[jax.ShapeDtypeStruct((M, w2), F32), jax.ShapeDtypeStruct((1, HEAD_DIM), F32)],
                          compiler_params=_params())(kv, gk.reshape(1, HEAD_DIM), dk, dv)


def _mem_tile(L):
    return _pick(L, 512, 8)


def _memattn_fwd(proj, khat, v, gq, name):
    L, W = proj.shape
    M, mw = khat.shape
    assert (W - mw) % mw == 0
    qcol = (W - mw) // mw
    tl = _mem_tile(L)
    scale = 1.0 / math.sqrt(HEAD_DIM)

    def body(q_ref, k_ref, v_ref, g_ref, o_ref):
        for h in range(mw // HEAD_DIM):
            sl = slice(h * HEAD_DIM, (h + 1) * HEAD_DIM)
            xh, _ = _head_norm(q_ref[:, sl])
            qh = (xh * g_ref[...]).astype(BF16)
            s = lax.dot_general(qh, k_ref[:, sl].astype(BF16), (((1,), (1,)), ((), ())),
                                preferred_element_type=F32) * scale
            e = jnp.exp(s - jnp.max(s, axis=-1, keepdims=True))
            p = e / jnp.sum(e, axis=-1, keepdims=True)
            o_ref[:, sl] = jnp.dot(p.astype(BF16), v_ref[:, sl].astype(BF16), preferred_element_type=F32)

    whole = lambda a: pl.BlockSpec(a.shape, lambda i: (0,) * a.ndim)
    g2 = gq.reshape(1, HEAD_DIM)
    return pl.pallas_call(
        body, name=name, grid=(L // tl,),
        in_specs=[pl.BlockSpec((tl, mw), lambda i: (i, qcol)), whole(khat), whole(v), whole(g2)],
        out_specs=pl.BlockSpec((tl, mw), lambda i: (i, 0)),
        out_shape=jax.ShapeDtypeStruct((L, mw), F32),
        compiler_params=_params("parallel"),
    )(proj, khat, v, g2)


def _memattn_bwd(proj, khat, v, gq, dcat, name):
    L, W = proj.shape
    M, mw = khat.shape
    qcol = (W - mw) // mw
    dcol = (dcat.shape[1] - mw) // mw
    tl = _mem_tile(L)
    scale = 1.0 / math.sqrt(HEAD_DIM)

    def body(q_ref, k_ref, v_ref, g_ref, do_ref, dq_ref, dk_ref, dv_ref, dg_ref):
        i = pl.program_id(0)

        @pl.when(i == 0)
        def _():
            dk_ref[...] = jnp.zeros_like(dk_ref)
            dv_ref[...] = jnp.zeros_like(dv_ref)
            dg_ref[...] = jnp.zeros_like(dg_ref)

        for h in range(mw // HEAD_DIM):
            sl = slice(h * HEAD_DIM, (h + 1) * HEAD_DIM)
            xh, r = _head_norm(q_ref[:, sl])
            qh = (xh * g_ref[...]).astype(BF16)
            kb = k_ref[:, sl].astype(BF16)
            vb = v_ref[:, sl].astype(BF16)
            dob = do_ref[:, sl].astype(BF16)
            s = lax.dot_general(qh, kb, (((1,), (1,)), ((), ())), preferred_element_type=F32) * scale
            e = jnp.exp(s - jnp.max(s, axis=-1, keepdims=True))
            p = e / jnp.sum(e, axis=-1, keepdims=True)
            dv_ref[:, sl] += lax.dot_general(p.astype(BF16), dob, (((0,), (0,)), ((), ())),
                                             preferred_element_type=F32)
            dp = lax.dot_general(dob, vb, (((1,), (1,)), ((), ())), preferred_element_type=F32)
            ds = (p * (dp - jnp.sum(dp * p, axis=-1, keepdims=True)) * scale).astype(BF16)
            dqh = jnp.dot(ds, kb, preferred_element_type=F32)
            dk_ref[:, sl] += lax.dot_general(ds, qh, (((0,), (0,)), ((), ())), preferred_element_type=F32)
            dg_ref[...] += jnp.sum(dqh * xh, axis=0, keepdims=True)
            dxh = dqh * g_ref[...]
            dq_ref[:, sl] = r * (dxh - xh * jnp.mean(dxh * xh, axis=-1, keepdims=True))

    whole = lambda a: pl.BlockSpec(a.shape, lambda i: (0,) * a.ndim)
    g2 = gq.reshape(1, HEAD_DIM)
    kshape = jax.ShapeDtypeStruct((M, mw), F32)
    return pl.pallas_call(
        body, name=name, grid=(L // tl,),
        in_specs=[pl.BlockSpec((tl, mw), lambda i: (i, qcol)), whole(khat), whole(v), whole(g2),
                  pl.BlockSpec((tl, mw), lambda i: (i, dcol))],
        out_specs=[pl.BlockSpec((tl, mw), lambda i: (i, 0)), whole(kshape), whole(kshape),
                   pl.BlockSpec((1, HEAD_DIM), lambda i: (0, 0))],
        out_shape=[jax.ShapeDtypeStruct((L, mw), F32), kshape, kshape, jax.ShapeDtypeStruct((1, HEAD_DIM), F32)],
        compiler_params=_params("arbitrary"),
    )(proj, khat, v, g2, dcat)


def _cmul(ar, ai, br, bi):
    return ar * br - ai * bi, ar * bi + ai * br


def _s5_discretize(log_dt, a_re, a_im):
    dt = jnp.exp(log_dt)
    mag = jnp.exp(a_re * dt)
    ab_re, ab_im = mag * jnp.cos(a_im * dt), mag * jnp.sin(a_im * dt)
    den = a_re * a_re + a_im * a_im
    inv_re, inv_im = a_re / den, -a_im / den
    f_re, f_im = _cmul(ab_re - 1.0, ab_im, inv_re, inv_im)
    return dt, ab_re, ab_im, inv_re, inv_im, f_re, f_im


def _s5_prep_fwd(log_dt, a_re, a_im, bt_re, bt_im, name):
    G, N = a_re.shape
    C = bt_re.shape[0]

    def body(ld_ref, ar_ref, ai_ref, br_ref, bi_ref, pr_ref, pi_ref, bbr_ref, bbi_ref):
        _, ab_re, ab_im, _, _, f_re, f_im = _s5_discretize(ld_ref[...], ar_ref[...], ai_ref[...])
        p_re, p_im = ab_re, ab_im
        for e in range(8):
            pr_ref[e] = p_re
            pi_ref[e] = p_im
            p_re, p_im = _cmul(p_re, p_im, ab_re, ab_im)
        for ch in range(C):
            bbr_ref[ch], bbi_ref[ch] = _cmul(f_re, f_im, br_ref[ch], bi_ref[ch])

    pw = jax.ShapeDtypeStruct((8, G, N), F32)
    bb = jax.ShapeDtypeStruct((C, G, N), F32)
    return pl.pallas_call(body, name=name, out_shape=[pw, pw, bb, bb], compiler_params=_params())(
        log_dt.reshape(G, 1), a_re, a_im, bt_re, bt_im)


def _s5_prep_bwd(log_dt, a_re, a_im, bt_re, bt_im, ga_re, ga_im, gbb_re, gbb_im, name):
    G, N = a_re.shape
    C = bt_re.shape[0]

    def body(ld_ref, ar_ref, ai_ref, br_ref, bi_ref, gar_ref, gai_ref, gbr_ref, gbi_ref,
             gld_ref, gare_ref, gaim_ref, gbtr_ref, gbti_ref):
        a_re_, a_im_ = ar_ref[...], ai_ref[...]
        dt, ab_re, ab_im, inv_re, inv_im, f_re, f_im = _s5_discretize(ld_ref[...], a_re_, a_im_)
        gf_re, gf_im = jnp.zeros((G, N), F32), jnp.zeros((G, N), F32)
        for ch in range(C):
            gbr, gbi = gbr_ref[ch], gbi_ref[ch]
            gbtr_ref[ch], gbti_ref[ch] = _cmul(f_re, -f_im, gbr, gbi)
            tr, ti = _cmul(br_ref[ch], -bi_ref[ch], gbr, gbi)
            gf_re, gf_im = gf_re + tr, gf_im + ti
        t_re, t_im = _cmul(gf_re, gf_im, inv_re, -inv_im)
        gab_re, gab_im = gar_ref[...] + t_re, gai_ref[...] + t_im
        u_re, u_im = _cmul(dt * ab_re, -dt * ab_im, gab_re, gab_im)
        fl_re, fl_im = _cmul(f_re, f_im, inv_re, inv_im)
        w_re, w_im = _cmul(fl_re, -fl_im, gf_re, gf_im)
        gare_ref[...] = u_re - w_re
        gaim_ref[...] = u_im - w_im
        la_re, la_im = _cmul(a_re_, a_im_, ab_re, ab_im)
        gdt = jnp.sum(la_re * gab_re + la_im * gab_im, axis=1, keepdims=True)
        gld_ref[...] = dt * gdt

    gn = jax.ShapeDtypeStruct((G, N), F32)
    bb = jax.ShapeDtypeStruct((C, G, N), F32)
    return pl.pallas_call(body, name=name,
                          out_shape=[jax.ShapeDtypeStruct((G, 1), F32), gn, gn, bb, bb],
                          compiler_params=_params())(
        log_dt.reshape(G, 1), a_re, a_im, bt_re, bt_im, ga_re, ga_im, gbb_re, gbb_im)


def _dot3(a, b, dims=(((1,), (0,)), ((), ()))):
    ah, al = _split_bf16(a)
    bh, bl = _split_bf16(b)
    d = lambda p, q: lax.dot_general(p, q, dims, preferred_element_type=F32)
    return d(ah, bh) + d(ah, bl) + d(al, bh)


def _block_diag(t):
    nch, P, r, c = t.shape
    eye = jnp.eye(P, dtype=t.dtype)
    return (t[:, :, :, None, :] * eye[None, :, None, :, None]).reshape(nch, P * r, P * c)


def _block_diag_extract(m, P):
    nch, R, Cc = m.shape
    r, c = R // P, Cc // P
    m5 = m.reshape(nch, P, r, P, c)
    return jnp.stack([m5[:, k, :, k, :] for k in range(P)], axis=1)


def _s5_tables(pw_re, pw_im):
    pr = pw_re.reshape(8, -1)
    pi = pw_im.reshape(8, -1)
    row = jnp.arange(8)[:, None]
    tiles = []
    for d in (1, 2, 4):
        keep = row >= d
        tiles += [jnp.where(keep, pr[d - 1][None], 0.0), jnp.where(keep, pi[d - 1][None], 0.0)]
    tiles += [pr, pi]
    for d in (1, 2, 4):
        keep = row <= 7 - d
        tiles += [jnp.where(keep, pr[d - 1][None], 0.0), jnp.where(keep, -pi[d - 1][None], 0.0)]
    tiles += [pr[::-1], -pi[::-1]]
    return jnp.stack(tiles, axis=0)


def _scan_tile(sr, si, m, shifts):
    for n, sh in enumerate(shifts):
        rr, ri = pltpu.roll(sr, sh, 0), pltpu.roll(si, sh, 0)
        mr, mi = m[2 * n], m[2 * n + 1]
        sr, si = sr + mr * rr - mi * ri, si + mr * ri + mi * rr
    return sr, si


def _s5_rows(L):
    return _pick(L, 512, 8)


def _s5_fwd(proj, bd_re, bd_im, cd_re, cd_imn, tables, d_skip, tok_w, name):
    L = proj.shape[0]
    nch = tok_w // LANE
    ns = S5_PACK * S5_STATE
    tl = _s5_rows(L)
    nr = L // tl

    def body(u_ref, bdr_ref, bdi_ref, cdr_ref, cdi_ref, tab_ref, d_ref, y_ref, xr_ref, xi_ref, carry):
        r = pl.program_id(1)

        @pl.when(r == 0)
        def _():
            carry[...] = jnp.zeros_like(carry)

        u = u_ref[...]
        xr_ref[...] = _dot3(u, bdr_ref[0])
        xi_ref[...] = _dot3(u, bdi_ref[0])
        m = [tab_ref[n] for n in range(8)]

        def step(t, c):
            cr, ci = c
            rows = pl.ds(pl.multiple_of(t * 8, 8), 8)
            sr, si = _scan_tile(xr_ref[rows, :], xi_ref[rows, :], m, (1, 2, 4))
            sr, si = sr + m[6] * cr - m[7] * ci, si + m[6] * ci + m[7] * cr
            xr_ref[rows, :] = sr
            xi_ref[rows, :] = si
            return jnp.broadcast_to(sr[7:8, :], (8, ns)), jnp.broadcast_to(si[7:8, :], (8, ns))

        cr, ci = lax.fori_loop(0, tl // 8, step, (carry[0], carry[1]))
        carry[0] = cr
        carry[1] = ci
        y_ref[...] = _dot3(xr_ref[...], cdr_ref[0]) + _dot3(xi_ref[...], cdi_ref[0]) + d_ref[...] * u

    chunk3 = lambda a: pl.BlockSpec((1,) + a.shape[1:], lambda j, r: (j, 0, 0))
    return pl.pallas_call(
        body, name=name, grid=(nch, nr),
        in_specs=[pl.BlockSpec((tl, LANE), lambda j, r: (r, j)),
                  chunk3(bd_re), chunk3(bd_im), chunk3(cd_re), chunk3(cd_imn),
                  pl.BlockSpec((8, 8, ns), lambda j, r: (0, 0, j)),
                  pl.BlockSpec((1, LANE), lambda j, r: (0, j))],
        out_specs=[pl.BlockSpec((tl, LANE), lambda j, r: (r, j)),
                   pl.BlockSpec((tl, ns), lambda j, r: (r, j)),
                   pl.BlockSpec((tl, ns), lambda j, r: (r, j))],
        out_shape=[jax.ShapeDtypeStruct((L, tok_w), F32),
                   jax.ShapeDtypeStruct((L, nch * ns), F32), jax.ShapeDtypeStruct((L, nch * ns), F32)],
        scratch_shapes=[pltpu.VMEM((2, 8, ns), F32)],
        compiler_params=_params("parallel", "arbitrary"),
    )(proj, bd_re, bd_im, cd_re, cd_imn, tables, d_skip.reshape(1, tok_w))


def _s5_bwd(proj, dy, xr, xi, bd_re, bd_im, cd_re, cd_imn, tables, d_skip, tok_w, name):
    L = proj.shape[0]
    nch = tok_w // LANE
    ns = S5_PACK * S5_STATE
    tl = _s5_rows(L)
    nr = L // tl
    tn = (((0,), (0,)), ((), ()))
    nt = (((1,), (1,)), ((), ()))

    def body(u_ref, dy_ref, xr_ref, xi_ref, bdr_ref, bdi_ref, cdr_ref, cdi_ref, tab_ref, d_ref,
             du_ref, gcr_ref, gci_ref, gbr_ref, gbi_ref, gar_ref, gai_ref, gd_ref, lr, li, carry):
        r = pl.program_id(1)

        @pl.when(r == 0)
        def _():
            carry[...] = jnp.zeros_like(carry)
            for ref in (gcr_ref, gci_ref, gbr_ref, gbi_ref, gar_ref, gai_ref, gd_ref):
                ref[...] = jnp.zeros_like(ref)

        u, dyb = u_ref[...], dy_ref[...]
        lr[...] = _dot3(dyb, cdr_ref[0], nt)
        li[...] = _dot3(dyb, cdi_ref[0], nt)
        m = [tab_ref[n] for n in range(8)]
        last = lax.broadcasted_iota(jnp.int32, (8, ns), 0) == 7

        def step(it, c):
            cr, ci, ar, ai = c
            t = tl // 8 - 1 - it
            rows = pl.ds(pl.multiple_of(t * 8, 8), 8)
            sr, si = _scan_tile(lr[rows, :], li[rows, :], m, (7, 6, 4))
            sr, si = sr + m[6] * cr - m[7] * ci, si + m[6] * ci + m[7] * cr
            lr[rows, :] = sr
            li[rows, :] = si
            nr_, ni_ = jnp.where(last, cr, pltpu.roll(sr, 7, 0)), jnp.where(last, ci, pltpu.roll(si, 7, 0))
            xr_t, xi_t = xr_ref[rows, :], xi_ref[rows, :]
            ar = ar + xr_t * nr_ + xi_t * ni_
            ai = ai + xr_t * ni_ - xi_t * nr_
            return jnp.broadcast_to(sr[0:1, :], (8, ns)), jnp.broadcast_to(si[0:1, :], (8, ns)), ar, ai

        z = jnp.zeros((8, ns), F32)
        cr, ci, ar, ai = lax.fori_loop(0, tl // 8, step, (carry[0], carry[1], z, z))
        carry[0] = cr
        carry[1] = ci
        gar_ref[...] += jnp.sum(ar, axis=0, keepdims=True)
        gai_ref[...] += jnp.sum(ai, axis=0, keepdims=True)
        lam_r, lam_i = lr[...], li[...]
        du_ref[...] = _dot3(lam_r, bdr_ref[0], nt) + _dot3(lam_i, bdi_ref[0], nt) + d_ref[...] * dyb
        gd_ref[...] += jnp.sum(dyb * u, axis=0, keepdims=True)
        gcr_ref[0] += _dot3(xr_ref[...], dyb, tn)
        gci_ref[0] += _dot3(xi_ref[...], dyb, tn)
        gbr_ref[0] += _dot3(u, lam_r, tn)
        gbi_ref[0] += _dot3(u, lam_i, tn)

    rev = lambda j, r: (nr - 1 - r, j)
    chunk3 = lambda a: pl.BlockSpec((1,) + a.shape[1:], lambda j, r: (j, 0, 0))
    gcd = jax.ShapeDtypeStruct((nch, ns, LANE), F32)
    gbd = jax.ShapeDtypeStruct((nch, LANE, ns), F32)
    gab = jax.ShapeDtypeStruct((1, nch * ns), F32)
    return pl.pallas_call(
        body, name=name, grid=(nch, nr),
        in_specs=[pl.BlockSpec((tl, LANE), rev), pl.BlockSpec((tl, LANE), rev),
                  pl.BlockSpec((tl, ns), rev), pl.BlockSpec((tl, ns), rev),
                  chunk3(bd_re), chunk3(bd_im), chunk3(cd_re), chunk3(cd_imn),
                  pl.BlockSpec((8, 8, ns), lambda j, r: (1, 0, j)),
                  pl.BlockSpec((1, LANE), lambda j, r: (0, j))],
        out_specs=[pl.BlockSpec((tl, LANE), rev), chunk3(gcd), chunk3(gcd), chunk3(gbd), chunk3(gbd),
                   pl.BlockSpec((1, ns), lambda j, r: (0, j)), pl.BlockSpec((1, ns), lambda j, r: (0, j)),
                   pl.BlockSpec((1, LANE), lambda j, r: (0, j))],
        out_shape=[jax.ShapeDtypeStruct((L, tok_w), F32), gcd, gcd, gbd, gbd, gab, gab,
                   jax.ShapeDtypeStruct((1, tok_w), F32)],
        scratch_shapes=[pltpu.VMEM((tl, ns), F32), pltpu.VMEM((tl, ns), F32), pltpu.VMEM((2, 8, ns), F32)],
        compiler_params=_params("parallel", "arbitrary"),
    )(proj, dy, xr, xi, bd_re, bd_im, cd_re, cd_imn, tables, d_skip.reshape(1, tok_w))


def _lin(dev):
    return 4 * dev[0] + 2 * dev[1] + dev[2]


def _shard_view(ref, axis, index, size):
    ix = tuple(pl.ds(index * size, size) if d == axis else pl.ds(0, ref.shape[d]) for d in range(len(ref.shape)))
    return ref.at[ix]


def _slot_index(dev, paired):
    idx = _lin(dev)
    half = N_DEV // 2
    return 2 * (idx % half) + idx // half if paired else idx


def _comm_sems(n):
    return [pltpu.SemaphoreType.DMA((n, 7)), pltpu.SemaphoreType.DMA((n, 7)), pltpu.SemaphoreType.DMA((n,))]


def _gather_side(shards, axes, paired=None):
    n = len(shards)
    paired = [False] * n if paired is None else paired
    sizes = [s.shape[a] for s, a in zip(shards, axes)]

    def tools(ins, outs, sems):
        send_sems, recv_sems, local_sems = sems
        x, y, c = lax.axis_index("x"), lax.axis_index("y"), lax.axis_index("c")
        me, sibling = (x, y, c), (x, y, 1 - c)
        chips = [(1 - x, y), (x, 1 - y), (1 - x, 1 - y)]

        def slot(a, dev):
            return _shard_view(outs[a], axes[a], _slot_index(dev, paired[a]), sizes[a])

        def copy(a, k, block, to, src=None):
            return pltpu.make_async_remote_copy(
                src_ref=slot(a, block) if src is None else src, dst_ref=slot(a, block),
                send_sem=send_sems.at[a, k], recv_sem=recv_sems.at[a, k],
                device_id=to, device_id_type=MESH)

        mine = [pltpu.make_async_copy(ins[a], slot(a, me), local_sems.at[a]) for a in range(n)]
        first = []
        for a in range(n):
            first.append(copy(a, 0, me, sibling, src=ins[a]))
            first += [copy(a, 1 + j, me, (*chip, c), src=ins[a]) for j, chip in enumerate(chips)]
        return me, sibling, chips, c, copy, mine, first

    def start(ins, outs, sems):
        _, _, _, _, _, mine, first = tools(ins, outs, sems)
        for cp in mine + first:
            cp.start()

    def finish(ins, outs, sems):
        me, sibling, chips, c, copy, mine, first = tools(ins, outs, sems)
        passed = []
        for j, chip in enumerate(chips):
            for a in range(n):
                copy(a, 1 + j, (*chip, c), me).wait_recv()
                cp = copy(a, 4 + j, (*chip, c), sibling)
                cp.start()
                passed.append(cp)
        for a in range(n):
            copy(a, 0, sibling, me).wait_recv()
            for j, chip in enumerate(chips):
                copy(a, 4 + j, (*chip, 1 - c), me).wait_recv()
        for cp in first + passed:
            cp.wait_send()
        for cp in mine:
            cp.wait()

    out_shape = []
    for s, a in zip(shards, axes):
        shp = list(s.shape)
        shp[a] *= N_DEV
        out_shape.append(jax.ShapeDtypeStruct(tuple(shp), s.dtype))
    return _Side(list(shards), out_shape, _comm_sems(n), start, finish)


def _scatter_side(fulls, axes, paired=None):
    n = len(fulls)
    paired = [False] * n if paired is None else paired
    sizes = [f.shape[a] // N_DEV for f, a in zip(fulls, axes)]
    masks = [(mx, my, mc) for mx in (0, 1) for my in (0, 1) for mc in (0, 1)][1:]

    def copies(ins, outs, sems):
        send_sems, recv_sems, local_sems = sems
        x, y, c = lax.axis_index("x"), lax.axis_index("y"), lax.axis_index("c")
        me = (x, y, c)
        flip = lambda v, m: 1 - v if m else v
        piece = lambda a, dev: _shard_view(ins[a], axes[a], _slot_index(dev, paired[a]), sizes[a])
        local = [pltpu.make_async_copy(piece(a, me), outs[a].at[_lin(me)], local_sems.at[a]) for a in range(n)]
        sends = []
        for k, (mx, my, mc) in enumerate(masks):
            peer = (flip(x, mx), flip(y, my), flip(c, mc))
            for a in range(n):
                sends.append(pltpu.make_async_remote_copy(
                    src_ref=piece(a, peer), dst_ref=outs[a].at[_lin(me)],
                    send_sem=send_sems.at[a, k], recv_sem=recv_sems.at[a, k],
                    device_id=peer, device_id_type=MESH))
        return local, sends

    def start(ins, outs, sems):
        local, sends = copies(ins, outs, sems)
        for cp in local + sends:
            cp.start()

    def finish(ins, outs, sems):
        local, sends = copies(ins, outs, sems)
        for cp in sends:
            cp.wait()
        for cp in local:
            cp.wait()

    out_shape = []
    for f, a, sz in zip(fulls, axes, sizes):
        shp = list(f.shape)
        shp[a] = sz
        out_shape.append(jax.ShapeDtypeStruct((N_DEV,) + tuple(shp), f.dtype))
    return _Side(list(fulls), out_shape, _comm_sems(n), start, finish)


def _all_gather(shards, axes, name, paired=None):
    return _run_side(_gather_side(shards, axes, paired), name)


def _adamw_sum(w, m, v, parts, name):
    rows, cols = w.shape
    c1 = 1.0 / (1.0 - ADAM_B1 ** ADAM_STEP)
    c2 = 1.0 / (1.0 - ADAM_B2 ** ADAM_STEP)

    def fn(j, wb, mb, vb, *ps):
        g = ps[0].astype(F32)
        for p in ps[1:]:
            g = g + p.astype(F32)
        mn = ADAM_B1 * mb + (1.0 - ADAM_B1) * g
        vn = ADAM_B2 * vb + (1.0 - ADAM_B2) * (g * g)
        delta = -ADAM_LR * ((mn * c1) / (jnp.sqrt(vn * c2) + ADAM_EPS) + ADAM_WD * wb)
        return g, delta, mn, vn

    per_row = cols * (12 + 16 + N_DEV * parts.dtype.itemsize)
    tl = _row_tile(rows, per_row, budget=12 * 1024 * 1024)
    flat = parts.reshape(N_DEV * rows, cols)
    nblk = rows // tl
    ins = [w, m, v] + [(flat, d * nblk, 0) for d in range(N_DEV)]
    return _ew(fn, ins, [(cols, F32)] * 4, rows=rows, cols=cols, tl=tl, name=name)


_NAMES = ['ffn1_norm', 'ffn1_w_gu', 'ffn1_w_down', 'mix_norm', 'mem_norm', 'w_mem_kv', 'xq_norm', 'xk_norm',
          'w_out', 'ffn2_norm', 'ffn2_w_gu', 'ffn2_w_down', 'sb_w_in', 's5_w_in', 's5_log_dt', 's5_a_re',
          's5_a_im', 's5_b_re', 's5_b_im', 's5_c_re', 's5_c_im', 's5_d', 's5_w_glu']
_SHARDED = {'ffn1_w_gu': 1, 'ffn1_w_down': 0, 'w_mem_kv': 0, 'w_out': 0, 'ffn2_w_gu': 1, 'ffn2_w_down': 0,
            'sb_w_in': 1, 's5_w_in': 0, 's5_w_glu': 0}
_PER_LAYER = ['ffn1_w_gu', 'ffn1_w_down', 'w_mem_kv', 'w_out', 'ffn2_w_gu', 'ffn2_w_down']
_SMALL = ['ffn1_norm', 'mix_norm', 'mem_norm', 'xq_norm', 'xk_norm', 'ffn2_norm', 's5_log_dt', 's5_a_re',
          's5_a_im', 's5_b_re', 's5_b_im', 's5_c_re', 's5_c_im']


def _ffn_fwd(x, gain, w_gu, w_down, tag, side=None):
    h = _rmsnorm_fwd(x, gain, f"{tag}_norm")
    gu, s, extra = _mm_swiglu(h, w_gu, f"{tag}_gu", side)
    out = _mm(s, w_down, res=x, scale=0.5, name=f"{tag}_down")
    return out, (x, h, gu, s), extra


def _ffn_bwd(dout, dout16, saved, gain, w_gu, w_down, tag):
    x, h, gu, s = saved
    g_down = _mm(s, dout16, ta=True, scale=0.5, out_dtype=BF16, name=f"{tag}_gdown")
    dgu, (p_down,) = _mm_dswiglu(dout16, w_down, gu, f"{tag}_ds", _scatter_side([g_down], [0]))
    g_gu = _mm(h, dgu, ta=True, out_dtype=BF16, name=f"{tag}_ggu")
    dh, (p_gu,) = _mm(dgu, w_gu, tb=True, name=f"{tag}_dh", side=_scatter_side([g_gu], [1], [True]))
    dx, dx16, g_gain = _rmsnorm_bwd(dh, x, gain, dout, f"{tag}_dnorm")
    return dx, dx16, g_gain, p_gu, p_down


def _mem_fwd(mem, mem_gain, w_kv, gk, tag):
    mem_h = _rmsnorm_fwd(mem, mem_gain, f"{tag}_memnorm")
    kv = _mm(mem_h, w_kv, name=f"{tag}_memkv")
    khat, v = _memkv_fwd(kv, gk, f"{tag}_memk")
    return mem_h, kv, khat, v


def _s5_arrange(p, j, nch):
    bt_re = jnp.transpose(p['s5_b_re'][j], (2, 0, 1))
    bt_im = jnp.transpose(p['s5_b_im'][j], (2, 0, 1))
    return bt_re, bt_im


def _mixer_fwd(i, x, p, W, mem, tag, side=None):
    L, D = x.shape
    mw = D // 4
    tok_w = D - mw
    j = i // 2
    h = _rmsnorm_fwd(x, p['mix_norm'][i], f"{tag}_norm")
    mem_h, kv, khat, v = _mem_fwd(mem, p['mem_norm'][i], W['w_mem_kv'], p['xk_norm'][i], tag)
    sv = dict(x=x, h=h, mem_h=mem_h, kv=kv, khat=khat, v=v)
    proj = _mm(h, W['sb_w_in' if i % 2 == 0 else 's5_w_in'], name=f"{tag}_in", side=side)
    extra = ()
    if side is not None:
        proj, extra = proj
    if i % 2 == 0:
        tok, rowsum = _sb_fwd(proj, tok_w // HEAD_DIM, f"{tag}_sb")
        sv.update(rowsum=rowsum)
    else:
        nch = tok_w // LANE
        bt_re, bt_im = _s5_arrange(p, j, nch)
        pw_re, pw_im, bb_re, bb_im = _s5_prep_fwd(p['s5_log_dt'][j], p['s5_a_re'][j], p['s5_a_im'][j],
                                                  bt_re, bt_im, f"{tag}_s5prep")
        C, N = bb_re.shape[0], bb_re.shape[2]
        bd_re = _block_diag(jnp.swapaxes(bb_re, 0, 1).reshape(nch, S5_PACK, C, N))
        bd_im = _block_diag(jnp.swapaxes(bb_im, 0, 1).reshape(nch, S5_PACK, C, N))
        ct_re = jnp.swapaxes(p['s5_c_re'][j], 1, 2).reshape(nch, S5_PACK, N, C)
        ct_im = jnp.swapaxes(p['s5_c_im'][j], 1, 2).reshape(nch, S5_PACK, N, C)
        cd_re, cd_imn = _block_diag(ct_re), _block_diag(-ct_im)
        tables = _s5_tables(pw_re, pw_im)
        y, xr, xi = _s5_fwd(proj, bd_re, bd_im, cd_re, cd_imn, tables, W['s5_d'][j], tok_w, f"{tag}_s5")
        tl = _row_tile(L, tok_w * 12)
        yg = _ew(lambda jj, yb: (_gelu(yb),), [y], [(tok_w, BF16)], rows=L, cols=tok_w, tl=tl, name=f"{tag}_gelu")[0]
        pre = _mm(yg, W['s5_w_glu'], name=f"{tag}_glu")
        tok = _ew(lambda jj, yb, pb: (_gelu(yb) * jax.nn.sigmoid(pb),), [y, pre], [(tok_w, F32)],
                  rows=L, cols=tok_w, tl=tl, name=f"{tag}_gate")[0]
        sv.update(y=y, xr=xr, xi=xi, pre=pre, yg=yg, bt_re=bt_re, bt_im=bt_im, bd_re=bd_re, bd_im=bd_im,
                  cd_re=cd_re, cd_imn=cd_imn, tables=tables)
    cross = _memattn_fwd(proj, khat, v, p['xq_norm'][i], f"{tag}_mem")
    cat = jnp.concatenate([tok.astype(BF16), cross.astype(BF16)], axis=1)
    out = _mm(cat, W['w_out'], res=x, name=f"{tag}_out")
    sv.update(proj=proj, cat=cat)
    return out, sv, extra


def _mixer_bwd(i, dout, dout16, sv, p, W, mem, tag):
    x, h, proj, cat = sv['x'], sv['h'], sv['proj'], sv['cat']
    L, D = x.shape
    mw = D // 4
    tok_w = D - mw
    j = i // 2
    g, parts = {}, {}
    g_out = _mm(cat, dout16, ta=True, out_dtype=BF16, name=f"{tag}_gout")
    dcat, (parts['w_out'],) = _mm(dout16, W['w_out'], tb=True, name=f"{tag}_dcat",
                                  side=_scatter_side([g_out], [_SHARDED['w_out']]))
    dqm, dkhat, dv, g_xq = _memattn_bwd(proj, sv['khat'], sv['v'], p['xq_norm'][i], dcat, f"{tag}_dmem")
    g['xq_norm'] = g_xq
    dkv, g_xk = _memkv_bwd(sv['kv'], p['xk_norm'][i], dkhat, dv, f"{tag}_dmemk")
    g['xk_norm'] = g_xk
    late = {'w_mem_kv': _mm(sv['mem_h'], dkv, ta=True, out_dtype=BF16, name=f"{tag}_gmemkv")}
    dmem_h = _mm(dkv, W['w_mem_kv'], tb=True, name=f"{tag}_dmemh")
    _, _, g['mem_norm'] = _rmsnorm_bwd(dmem_h, mem, p['mem_norm'][i], None, f"{tag}_dmemnorm")
    if i % 2 == 0:
        dq, dk, dvv = _sb_bwd(proj, dcat, sv['rowsum'], tok_w // HEAD_DIM, f"{tag}_dsb")
        dproj = jnp.concatenate([dq.astype(BF16), dk.astype(BF16), dvv.astype(BF16), dqm.astype(BF16)], axis=1)
        w_in = W['sb_w_in']
        late['sb_w_in'] = _mm(h, dproj, ta=True, out_dtype=BF16, name=f"{tag}_gin")
    else:
        y, pre, yg = sv['y'], sv['pre'], sv['yg']
        tl = _row_tile(L, tok_w * 20)

        def gate_bwd(jj, dt, yb, pb):
            gl = _gelu(yb)
            sg = jax.nn.sigmoid(pb)
            return dt * gl * sg * (1.0 - sg)

        dpre = _ew(gate_bwd, [dcat, y, pre], [(tok_w, BF16)], rows=L, cols=tok_w, tl=tl, name=f"{tag}_dgate")[0]
        late['s5_w_glu'] = _mm(yg, dpre, ta=True, out_dtype=BF16, name=f"{tag}_gglu")
        dyg = _mm(dpre, W['s5_w_glu'], tb=True, name=f"{tag}_dyg")

        def gelu_bwd(jj, dt, yb, pb, db):
            return (dt * jax.nn.sigmoid(pb) + db) * _gelu_grad(yb)

        dy = _ew(gelu_bwd, [dcat, y, pre, dyg], [(tok_w, F32)], rows=L, cols=tok_w, tl=tl, name=f"{tag}_dgelu")[0]
        du, gcd_re, gcd_imn, gbd_re, gbd_im, ga_re, ga_im, g_d = _s5_bwd(
            proj, dy, sv['xr'], sv['xi'], sv['bd_re'], sv['bd_im'], sv['cd_re'], sv['cd_imn'], sv['tables'],
            W['s5_d'][j], tok_w, f"{tag}_ds5")
        G, N = p['s5_a_re'].shape[1], p['s5_a_re'].shape[2]
        C = S5_GROUP
        gct_re = _block_diag_extract(gcd_re, S5_PACK).reshape(G, N, C)
        gct_im = -_block_diag_extract(gcd_imn, S5_PACK).reshape(G, N, C)
        g['s5_c_re'] = jnp.swapaxes(gct_re, 1, 2)
        g['s5_c_im'] = jnp.swapaxes(gct_im, 1, 2)
        gbb_re = jnp.swapaxes(_block_diag_extract(gbd_re, S5_PACK).reshape(G, C, N), 0, 1)
        gbb_im = jnp.swapaxes(_block_diag_extract(gbd_im, S5_PACK).reshape(G, C, N), 0, 1)
        g_ld, g_are, g_aim, gbt_re, gbt_im = _s5_prep_bwd(
            p['s5_log_dt'][j], p['s5_a_re'][j], p['s5_a_im'][j], sv['bt_re'], sv['bt_im'],
            ga_re.reshape(G, N), ga_im.reshape(G, N), gbb_re, gbb_im, f"{tag}_ds5prep")
        g['s5_log_dt'] = g_ld.reshape(G)
        g['s5_a_re'], g['s5_a_im'] = g_are, g_aim
        g['s5_b_re'], g['s5_b_im'] = jnp.transpose(gbt_re, (1, 2, 0)), jnp.transpose(gbt_im, (1, 2, 0))
        g['s5_d'] = g_d.reshape(tok_w)
        dproj = jnp.concatenate([du.astype(BF16), dqm.astype(BF16)], axis=1)
        w_in = W['s5_w_in']
        late['s5_w_in'] = _mm(h, dproj, ta=True, out_dtype=BF16, name=f"{tag}_gin")
    names = list(late)
    dh, bufs = _mm(dproj, w_in, tb=True, name=f"{tag}_dh",
                   side=_scatter_side([late[n] for n in names], [_SHARDED[n] for n in names]))
    parts.update(zip(names, bufs))
    dx, dx16, g['mix_norm'] = _rmsnorm_bwd(dh, x, p['mix_norm'][i], dout, f"{tag}_dnorm")
    return dx, dx16, g, parts


def _pack(arrs):
    flat = jnp.concatenate([a.reshape(-1).astype(F32) for a in arrs])
    pad = (-flat.shape[0]) % (512 * LANE)
    flat = jnp.pad(flat, (0, pad))
    return flat.reshape(-1, LANE)


def _unpack(flat, like):
    flat = flat.reshape(-1)
    out, off = [], 0
    for a in like:
        n = math.prod(a.shape)
        out.append(flat[off:off + n].reshape(a.shape))
        off += n
    return out


def kernel(x, mem, ffn1_norm, ffn1_w_gu, ffn1_w_down, mix_norm, mem_norm, w_mem_kv, xq_norm, xk_norm, w_out, ffn2_norm, ffn2_w_gu, ffn2_w_down, sb_w_in, s5_w_in, s5_log_dt, s5_a_re, s5_a_im, s5_b_re, s5_b_im, s5_c_re, s5_c_im, s5_d, s5_w_glu, loss_target, m_ffn1_norm, m_ffn1_w_gu, m_ffn1_w_down, m_mix_norm, m_mem_norm, m_w_mem_kv, m_xq_norm, m_xk_norm, m_w_out, m_ffn2_norm, m_ffn2_w_gu, m_ffn2_w_down, m_sb_w_in, m_s5_w_in, m_s5_log_dt, m_s5_a_re, m_s5_a_im, m_s5_b_re, m_s5_b_im, m_s5_c_re, m_s5_c_im, m_s5_d, m_s5_w_glu, v_ffn1_norm, v_ffn1_w_gu, v_ffn1_w_down, v_mix_norm, v_mem_norm, v_w_mem_kv, v_xq_norm, v_xk_norm, v_w_out, v_ffn2_norm, v_ffn2_w_gu, v_ffn2_w_down, v_sb_w_in, v_s5_w_in, v_s5_log_dt, v_s5_a_re, v_s5_a_im, v_s5_b_re, v_s5_b_im, v_s5_c_re, v_s5_c_im, v_s5_d, v_s5_w_glu):
    given = dict(locals())
    p = {n: given[n] for n in _NAMES}
    mom = {n: given["m_" + n] for n in _NAMES}
    var = {n: given["v_" + n] for n in _NAMES}
    depth = ffn1_norm.shape[0]
    xs, mems, target = x[0], mem[0], loss_target[0]
    L, D = xs.shape
    me = 4 * lax.axis_index("x") + 2 * lax.axis_index("y") + lax.axis_index("c")

    def layer_slice(name, i):
        return p[name][i // 2] if name in ('sb_w_in', 's5_w_in', 's5_w_glu') else p[name][i]

    def layer_names(i):
        return _PER_LAYER + (['sb_w_in'] if i % 2 == 0 else ['s5_w_in', 's5_w_glu'])

    n_b = s5_d.shape[0]
    d_rows = jnp.broadcast_to(s5_d.reshape(1, -1), (8, s5_d.size))
    d_full = _all_gather([d_rows], [0], "gather_s5d")[0][::8]
    d_full = jnp.swapaxes(d_full.reshape(N_DEV, n_b, -1), 0, 1).reshape(n_b, -1)

    def gather_group(i, names):
        shards = [_cast_bf16(layer_slice(nm, i), f"cast_{nm}") for nm in names]
        return _gather_side(shards, [_SHARDED[nm] for nm in names], [nm.endswith('w_gu') for nm in names])

    groups = lambda i: (['ffn1_w_gu', 'ffn1_w_down'], [nm for nm in layer_names(i) if 'ffn' not in nm],
                        ['ffn2_w_gu', 'ffn2_w_down'])
    W = {}
    for names in groups(0):
        W.update(zip(names, _run_side(gather_group(0, names), f"gather_first_{names[0]}")))
    Ws, saves = [], []
    act = xs
    for i in range(depth):
        W['s5_d'] = d_full
        Ws.append(W)
        nxt = [gather_group(i + 1, names) if i + 1 < depth else None for names in groups(i + 1)]
        act, s1, e1 = _ffn_fwd(act, p['ffn1_norm'][i], W['ffn1_w_gu'], W['ffn1_w_down'], "ffn1", nxt[0])
        act, s2, e2 = _mixer_fwd(i, act, p, W, mems, f"mix{i % 2}", nxt[1])
        act, s3, e3 = _ffn_fwd(act, p['ffn2_norm'][i], W['ffn2_w_gu'], W['ffn2_w_down'], "ffn2", nxt[2])
        saves.append((s1, s2, s3))
        W = {}
        if i + 1 < depth:
            for names, got in zip(groups(i + 1), (e1, e2, e3)):
                W.update(zip(names, got))

    loss_part, dact, dact16 = _loss_fwd_bwd(act, target, "loss")
    loss = lax.psum(loss_part[0, 0], ("x", "y", "c"))

    small_g = {n: [None] * p[n].shape[0] for n in _SMALL + ['s5_d']}
    big = {n: [None] * p[n].shape[0] for n in _SHARDED}
    for i in reversed(range(depth)):
        W = Ws[i]
        s1, s2, s3 = saves[i]
        j = i // 2
        parts = {}
        dact, dact16, gg, parts['ffn2_w_gu'], parts['ffn2_w_down'] = _ffn_bwd(
            dact, dact16, s3, p['ffn2_norm'][i], W['ffn2_w_gu'], W['ffn2_w_down'], "ffn2")
        small_g['ffn2_norm'][i] = gg
        dact, dact16, gm, pm = _mixer_bwd(i, dact, dact16, s2, p, W, mems, f"mix{i % 2}")
        parts.update(pm)
        dact, dact16, gg, parts['ffn1_w_gu'], parts['ffn1_w_down'] = _ffn_bwd(
            dact, dact16, s1, p['ffn1_norm'][i], W['ffn1_w_gu'], W['ffn1_w_down'], "ffn1")
        small_g['ffn1_norm'][i] = gg
        for n in _SMALL + ['s5_d']:
            if n in gm:
                small_g[n][j if n.startswith('s5_') else i] = gm[n]
        for nm in layer_names(i):
            li = j if nm in ('sb_w_in', 's5_w_in', 's5_w_glu') else i
            big[nm][li] = _adamw_sum(p[nm][li], mom[nm][li], var[nm][li], parts[nm], f"adamw_{nm}")

    small_names = _SMALL + ['s5_d']
    g_small = [jnp.stack([gi.reshape(p[n].shape[1:]) if n != 's5_d' else gi for gi in small_g[n]])
               for n in small_names]
    d_w = jnp.zeros_like(d_full)
    cw = s5_d.shape[1]
    w_small = [p[n] for n in _SMALL]
    m_small = [mom[n] for n in _SMALL]
    v_small = [var[n] for n in _SMALL]
    place = lambda a: lax.dynamic_update_slice(d_w, a, (0, me * cw))
    flat_g = _pack(g_small)
    flat_w = _pack(w_small + [place(s5_d)])
    flat_m = _pack(m_small + [place(mom['s5_d'])])
    flat_v = _pack(v_small + [place(var['s5_d'])])
    parts = _all_gather([flat_g.reshape((1,) + flat_g.shape)], [0], "gather_small")[0]
    res_small = _adamw_sum(flat_w, flat_m, flat_v, parts, "adamw_small")
    like = [p[n] for n in _SMALL] + [d_full]
    un = [_unpack(r, like) for r in res_small]

    def result(kind, n):
        if n in _SHARDED:
            return jnp.stack([r[kind] for r in big[n]])
        if n == 's5_d':
            return lax.dynamic_slice(un[kind][-1], (0, me * cw), (n_b, cw))
        return un[kind][_SMALL.index(n)]

    outs = [loss, dact.reshape((1,) + dact.shape)]
    for kind in range(4):
        outs += [result(kind, n) for n in _NAMES]
    return tuple(outs)
```

```python
import functools
import math

import jax
import jax.numpy as jnp
from jax import lax
from jax.experimental import pallas as pl
from jax.experimental.pallas import tpu as pltpu

F32 = jnp.float32
BF16 = jnp.bfloat16
MESH = pl.DeviceIdType.MESH

HEAD_DIM = 128
S5_GROUP = 16
S5_STATE = 64
S5_PACK = 8
EPS = 1e-6
ADAM_LR, ADAM_B1, ADAM_B2, ADAM_EPS, ADAM_WD, ADAM_STEP = 0.001, 0.9, 0.999, 1e-08, 0.01, 10
N_DEV = 8
VMEM_LIMIT_BYTES = 56 * 1024 * 1024
SB_TILE = 256
LANE = 128


def _params(*sem):
    if sem:
        return pltpu.CompilerParams(dimension_semantics=sem, vmem_limit_bytes=VMEM_LIMIT_BYTES)
    return pltpu.CompilerParams(vmem_limit_bytes=VMEM_LIMIT_BYTES)


def _pick(dim, target, align=LANE):
    if dim <= target:
        return dim
    best = None
    for t in range(align, target + 1, align):
        if dim % t == 0:
            best = t
    assert best is not None, (dim, target, align)
    return best


MM_OPERAND_BYTES = 20 * 1024 * 1024
ANY_SPEC = pl.BlockSpec(memory_space=pl.ANY)


class _Side:
    def __init__(self, ins, out_shape, scratch, start, finish):
        self.ins, self.out_shape, self.scratch, self.start, self.finish = ins, out_shape, scratch, start, finish


def _with_side(side, n_in, n_out, n_scratch, refs, first, last, compute):
    if side is None:
        compute(refs)
        return
    ns_in, ns_out = len(side.ins), len(side.out_shape)
    own_in, s_in = refs[:n_in], refs[n_in:n_in + ns_in]
    o0 = n_in + ns_in
    own_out, s_out = refs[o0:o0 + n_out], refs[o0 + n_out:o0 + n_out + ns_out]
    c0 = o0 + n_out + ns_out
    own_scr, s_scr = refs[c0:c0 + n_scratch], refs[c0 + n_scratch:]

    @pl.when(first)
    def _():
        side.start(s_in, s_out, s_scr)

    compute(tuple(own_in) + tuple(own_out) + tuple(own_scr))

    @pl.when(last)
    def _():
        side.finish(s_in, s_out, s_scr)


def _side_call(body_own, side, *, name, grid, in_specs, out_specs, out_shape, scratch, args, sem):
    n_in, n_out, n_scr = len(args), len(out_shape), len(scratch)
    if side is not None:
        in_specs = list(in_specs) + [ANY_SPEC] * len(side.ins)
        out_specs = list(out_specs) + [ANY_SPEC] * len(side.out_shape)
        out_shape = list(out_shape) + list(side.out_shape)
        scratch = list(scratch) + list(side.scratch)
        args = list(args) + list(side.ins)
        sem = ("arbitrary",) * len(grid)

    def body(*refs):
        ids = [pl.program_id(d) for d in range(len(grid))]
        first, last = ids[0] == 0, ids[0] == grid[0] - 1
        for d in range(1, len(grid)):
            first = jnp.logical_and(first, ids[d] == 0)
            last = jnp.logical_and(last, ids[d] == grid[d] - 1)
        _with_side(side, n_in, n_out, n_scr, refs, first, last, body_own)

    res = pl.pallas_call(body, name=name, grid=grid, in_specs=in_specs, out_specs=out_specs,
                         out_shape=out_shape, scratch_shapes=scratch, compiler_params=_params(*sem))(*args)
    return res[:n_out], res[n_out:]


def _run_side(side, name):
    def body(*refs):
        ni, no = len(side.ins), len(side.out_shape)
        side.start(refs[:ni], refs[ni:ni + no], refs[ni + no:])
        side.finish(refs[:ni], refs[ni:ni + no], refs[ni + no:])

    return pl.pallas_call(body, name=name, in_specs=[ANY_SPEC] * len(side.ins),
                          out_specs=[ANY_SPEC] * len(side.out_shape), out_shape=list(side.out_shape),
                          scratch_shapes=list(side.scratch))(*side.ins)


def _mm(a, b, *, ta=False, tb=False, out_dtype=F32, res=None, scale=None, name, side=None):
    if ta:
        K, M = a.shape
    else:
        M, K = a.shape
    if tb:
        N, Kb = b.shape
    else:
        Kb, N = b.shape
    assert K == Kb, (a.shape, b.shape, ta, tb)
    tm, tn = _pick(M, 1024), _pick(N, 1024)
    per_k = 2 * (tm * a.dtype.itemsize + tn * b.dtype.itemsize)
    tk = _pick(K, max(LANE, MM_OPERAND_BYTES // per_k))
    nk = K // tk
    dims = (((0 if ta else 1,), (1 if tb else 0,)), ((), ()))

    def finish(o, r_ref, o_ref):
        if scale is not None:
            o = o * scale
        if r_ref is not None:
            o = r_ref[...].astype(F32) + o
        o_ref[...] = o.astype(o_ref.dtype)

    def compute(refs):
        a_ref, b_ref = refs[0], refs[1]
        r_ref = refs[2] if res is not None else None
        o_ref = refs[3] if res is not None else refs[2]
        part = lax.dot_general(a_ref[...].astype(BF16), b_ref[...].astype(BF16), dims, preferred_element_type=F32)
        if nk == 1:
            finish(part, r_ref, o_ref)
            return
        acc = refs[-1]
        k = pl.program_id(2)

        @pl.when(k == 0)
        def _():
            acc[...] = part

        @pl.when(k > 0)
        def _():
            acc[...] += part

        @pl.when(k == nk - 1)
        def _():
            finish(acc[...], r_ref, o_ref)

    a_spec = pl.BlockSpec((tk, tm), lambda i, j, k: (k, i)) if ta else pl.BlockSpec((tm, tk), lambda i, j, k: (i, k))
    b_spec = pl.BlockSpec((tn, tk), lambda i, j, k: (j, k)) if tb else pl.BlockSpec((tk, tn), lambda i, j, k: (k, j))
    o_spec = pl.BlockSpec((tm, tn), lambda i, j, k: (i, j))
    in_specs, args = [a_spec, b_spec], [a, b]
    if res is not None:
        in_specs.append(o_spec)
        args.append(res)
    own, extra = _side_call(
        compute, side, name=name, grid=(M // tm, N // tn, nk), in_specs=in_specs, out_specs=[o_spec],
        out_shape=[jax.ShapeDtypeStruct((M, N), out_dtype)],
        scratch=[pltpu.VMEM((tm, tn), F32)] if nk > 1 else [], args=args,
        sem=("parallel", "parallel", "arbitrary"))
    return own[0] if side is None else (own[0], extra)


def _silu_grads(g):
    sg = jax.nn.sigmoid(g)
    return g * sg, sg * (1.0 + g * (1.0 - sg))


def _mm_swiglu(h, w_gu, name, side=None):
    M, K = h.shape
    n2 = w_gu.shape[1]
    c = n2 // N_DEV
    tm = _pick(M, 512)

    def compute(refs):
        h_ref, w_ref, gu_ref, s_ref = refs
        r = jnp.dot(h_ref[...], w_ref[...], preferred_element_type=F32)
        gu_ref[...] = r.astype(gu_ref.dtype)
        act, _ = _silu_grads(r[:, :c])
        s_ref[...] = (act * r[:, c:]).astype(s_ref.dtype)

    own, extra = _side_call(
        compute, side, name=name, grid=(N_DEV // 2, M // tm),
        in_specs=[pl.BlockSpec((tm, K), lambda j, i: (i, 0)), pl.BlockSpec((K, 2 * c), lambda j, i: (0, j))],
        out_specs=[pl.BlockSpec((tm, 2 * c), lambda j, i: (i, j)), pl.BlockSpec((tm, c), lambda j, i: (i, j))],
        out_shape=[jax.ShapeDtypeStruct((M, n2), BF16), jax.ShapeDtypeStruct((M, n2 // 2), BF16)],
        scratch=[], args=[h, w_gu], sem=("parallel", "parallel"))
    return own[0], own[1], extra


def _mm_dswiglu(dout, w_down, gu, name, side=None):
    M, D = dout.shape
    F_ = w_down.shape[0]
    c = F_ // (N_DEV // 2)
    tm = _pick(M, 512)
    nt = (((1,), (1,)), ((), ()))

    def compute(refs):
        d_ref, w_ref, gu_ref, o_ref = refs
        ds = 0.5 * lax.dot_general(d_ref[...], w_ref[...], nt, preferred_element_type=F32)
        g, u = gu_ref[:, :c].astype(F32), gu_ref[:, c:].astype(F32)
        act, dact = _silu_grads(g)
        o_ref[:, :c] = (ds * u * dact).astype(o_ref.dtype)
        o_ref[:, c:] = (ds * act).astype(o_ref.dtype)

    own, extra = _side_call(
        compute, side, name=name, grid=(N_DEV // 2, M // tm),
        in_specs=[pl.BlockSpec((tm, D), lambda j, i: (i, 0)), pl.BlockSpec((c, D), lambda j, i: (j, 0)),
                  pl.BlockSpec((tm, 2 * c), lambda j, i: (i, j))],
        out_specs=[pl.BlockSpec((tm, 2 * c), lambda j, i: (i, j))],
        out_shape=[jax.ShapeDtypeStruct((M, 2 * F_), BF16)],
        scratch=[], args=[dout, w_down, gu], sem=("parallel", "parallel"))
    return own[0], extra


class _Whole:
    def __init__(self, a):
        self.a = a


def _ew(fn, ins, outs, accs=(), *, rows, cols, tl, tc=None, name):
    tc = cols if tc is None else tc
    nj, ni = cols // tc, rows // tl
    assert nj * tc == cols and ni * tl == rows, (rows, cols, tl, tc)
    in_specs, args = [], []
    for it in ins:
        if isinstance(it, _Whole):
            in_specs.append(pl.BlockSpec(it.a.shape, lambda j, i, _n=it.a.ndim: (0,) * _n))
            args.append(it.a)
        else:
            arr, ro, co = it if isinstance(it, tuple) else (it, 0, 0)
            in_specs.append(pl.BlockSpec((tl, tc), lambda j, i, _ro=ro, _co=co: (i + _ro, j + _co)))
            args.append(arr)
    out_shape = [jax.ShapeDtypeStruct((rows, c), dt) for c, dt in outs]
    out_specs = [pl.BlockSpec((tl, tc), lambda j, i: (i, j)) for _ in outs]
    for c, t in accs:
        out_shape.append(jax.ShapeDtypeStruct((1, c), F32))
        out_specs.append(pl.BlockSpec((1, t), lambda j, i: (0, j)))
    n_in, n_out = len(args), len(outs)

    def body(*refs):
        j, i = pl.program_id(0), pl.program_id(1)
        vals = fn(j, *[r[...] for r in refs[:n_in]])
        if not isinstance(vals, (tuple, list)):
            vals = (vals,)
        for r, v in zip(refs[n_in:n_in + n_out], vals[:n_out]):
            r[...] = v.astype(r.dtype)
        for r, v in zip(refs[n_in + n_out:], vals[n_out:]):
            @pl.when(i == 0)
            def _(r=r):
                r[...] = jnp.zeros_like(r)
            r[...] += v

    res = pl.pallas_call(
        body, name=name, grid=(nj, ni), in_specs=in_specs, out_specs=out_specs, out_shape=out_shape,
        compiler_params=_params("parallel", "arbitrary"),
    )(*args)
    return res


def _row_tile(rows, bytes_per_row, budget=10 * 1024 * 1024, align=16):
    cap = max(align, budget // max(1, bytes_per_row))
    best = None
    for t in range(align, min(rows, cap) + 1, align):
        if rows % t == 0:
            best = t
    if best is None:
        best = rows
    return best


def _cast_bf16(w2d, name):
    rows, cols = w2d.shape
    tl = _row_tile(rows, cols * 6)
    return _ew(lambda j, w: w, [w2d], [(cols, BF16)], rows=rows, cols=cols, tl=tl, name=name)[0]


def _rmsnorm_fwd(x, g, name, out_dtype=BF16):
    rows, cols = x.shape

    def fn(j, xb, gb):
        r = lax.rsqrt(jnp.mean(xb * xb, axis=-1, keepdims=True) + EPS)
        return (xb * r * gb,)

    tl = _row_tile(rows, cols * 12)
    return _ew(fn, [x, _Whole(g.reshape(1, cols))], [(cols, out_dtype)], rows=rows, cols=cols, tl=tl, name=name)[0]


def _rmsnorm_bwd(dh, x, g, dres, name):
    rows, cols = x.shape

    def fn(j, dhb, xb, gb, *rest):
        r = lax.rsqrt(jnp.mean(xb * xb, axis=-1, keepdims=True) + EPS)
        xh = xb * r
        dxh = dhb * gb
        dx = r * (dxh - xh * jnp.mean(dxh * xh, axis=-1, keepdims=True))
        if rest:
            dx = dx + rest[0]
        return dx, dx, jnp.sum(dhb * xh, axis=0, keepdims=True)

    ins = [dh, x, _Whole(g.reshape(1, cols))] + ([dres] if dres is not None else [])
    tl = _row_tile(rows, cols * 28)
    return _ew(fn, ins, [(cols, F32), (cols, BF16)], [(cols, cols)], rows=rows, cols=cols, tl=tl, name=name)


_GELU_K = math.sqrt(2.0 / math.pi)
_GELU_C = 0.044715


def _gelu(y):
    return 0.5 * y * (1.0 + jnp.tanh(_GELU_K * (y + _GELU_C * y * y * y)))


def _gelu_grad(y):
    t = jnp.tanh(_GELU_K * (y + _GELU_C * y * y * y))
    return 0.5 * (1.0 + t) + 0.5 * y * (1.0 - t * t) * _GELU_K * (1.0 + 3.0 * _GELU_C * y * y)


def _loss_fwd_bwd(y, target, name):
    rows, cols = y.shape

    def fn(j, yb, tb):
        e = yb - tb
        part = 0.5 * jnp.sum(e * e) / cols
        dy = e * (1.0 / cols)
        return dy, dy, jnp.full((1, LANE), part, F32)

    tl = _row_tile(rows, cols * 20)
    dy, dy16, loss = _ew(fn, [y, target], [(cols, F32), (cols, BF16)], [(LANE, LANE)], rows=rows, cols=cols,
                         tl=tl, name=name)
    return loss, dy, dy16


SB_SCALE = 1.0 / math.sqrt(HEAD_DIM)
SB_ROWS = 4


def _sb_terms(z):
    z2 = z * (SB_SCALE / math.log(2.0))
    t = jnp.log2(1.0 + jnp.exp2(-jnp.abs(z2)))
    return jnp.maximum(z2, 0.0) + t, jnp.minimum(z2, 0.0) - t


def _strict_lower(n, m):
    return lax.broadcasted_iota(jnp.int32, (n, m), 0) > lax.broadcasted_iota(jnp.int32, (n, m), 1)


def _sb_tiles(L):
    T = min(SB_TILE, L)
    R = max(r for r in (SB_ROWS, 2, 1) if L % (r * T) == 0)
    return T, R


def _split_bf16(v):
    hi = v.astype(BF16)
    lo = (v - hi.astype(F32)).astype(BF16)
    return hi, lo


def _sb_fwd(proj, n_heads, name):
    L = proj.shape[0]
    T, R = _sb_tiles(L)
    M = R * T
    H = n_heads
    nt = (((1,), (1,)), ((), ()))

    def body(q_ref, k_ref, v_ref, o_ref, s_ref, acc_ref, later_ref):
        i = pl.program_id(1)
        acc_ref[...] = jnp.zeros_like(acc_ref)
        later_ref[...] = jnp.zeros_like(later_ref)
        upper = _strict_lower(T, T).astype(BF16)

        def strip(row0, nrows, jb, diagonal):
            rows = slice(row0, row0 + nrows)
            start = pl.multiple_of(jb * T, T)
            kb, vb = k_ref[pl.ds(start, T), :], v_ref[pl.ds(start, T), :]
            z = lax.dot_general(q_ref[rows, :], kb, nt, preferred_element_type=F32)
            sp, logsig = _sb_terms(z)
            mask = _strict_lower(nrows, T) if diagonal else None
            spb = (jnp.where(mask, sp, 0.0) if diagonal else sp).astype(BF16)
            between = jnp.dot(spb, upper, preferred_element_type=F32)
            later = later_ref[rows, :]
            w = jnp.exp2(logsig - between - later)
            if diagonal:
                w = jnp.where(mask, w, 0.0)
            acc_ref[rows, :] += jnp.dot(w.astype(BF16), vb, preferred_element_type=F32)
            later_ref[rows, :] = later + between[:, 0:1] + spb[:, 0:1].astype(F32)

        for d in reversed(range(R)):
            strip(d * T, M - d * T, R * i + d, True)

        def step(it, carry):
            strip(0, M, R * i - 1 - it, False)
            return carry

        lax.fori_loop(0, R * i, step, 0)
        o_ref[...] = acc_ref[...]
        s_ref[...] = jnp.broadcast_to(later_ref[...], (M, HEAD_DIM))

    return pl.pallas_call(
        body, name=name, grid=(H, L // M),
        in_specs=[pl.BlockSpec((M, HEAD_DIM), lambda h, i: (i, h)),
                  pl.BlockSpec((L, HEAD_DIM), lambda h, i: (0, H + h)),
                  pl.BlockSpec((L, HEAD_DIM), lambda h, i: (0, 2 * H + h))],
        out_specs=[pl.BlockSpec((M, HEAD_DIM), lambda h, i: (i, h)),
                   pl.BlockSpec((M, HEAD_DIM), lambda h, i: (i, h))],
        out_shape=[jax.ShapeDtypeStruct((L, H * HEAD_DIM), F32), jax.ShapeDtypeStruct((L, H * HEAD_DIM), F32)],
        scratch_shapes=[pltpu.VMEM((M, HEAD_DIM), F32), pltpu.VMEM((M, 1), F32)],
        compiler_params=_params("parallel", "arbitrary"),
    )(proj, proj, proj)


def _sb_bwd(proj, do, rowsum, n_heads, name):
    L = proj.shape[0]
    T, R = _sb_tiles(L)
    M = R * T
    H = n_heads
    nt = (((1,), (1,)), ((), ()))
    tn = (((0,), (0,)), ((), ()))

    def body(q_ref, k_ref, v_ref, do_ref, s_ref, dq_ref, dk_ref, dv_ref, do16_ref, upto_ref, hsum_ref):
        i = pl.program_id(1)

        @pl.when(i == 0)
        def _():
            dk_ref[...] = jnp.zeros_like(dk_ref)
            dv_ref[...] = jnp.zeros_like(dv_ref)

        dq_ref[...] = jnp.zeros_like(dq_ref)
        upto_ref[...] = jnp.zeros_like(upto_ref)
        hsum_ref[...] = jnp.zeros_like(hsum_ref)
        do16_ref[...] = do_ref[...].astype(BF16)
        upper = _strict_lower(T, T).astype(BF16)
        lower_inc = jnp.logical_not(_strict_lower(T, T)).astype(BF16)

        def strip(row0, nrows, jb, diagonal):
            rows = slice(row0, row0 + nrows)
            start = pl.multiple_of(jb * T, T)
            kb, vb = k_ref[pl.ds(start, T), :], v_ref[pl.ds(start, T), :]
            q, dob = q_ref[rows, :], do16_ref[rows, :]
            z = lax.dot_general(q, kb, nt, preferred_element_type=F32)
            sp, logsig = _sb_terms(z)
            mask = _strict_lower(nrows, T) if diagonal else None
            spb = (jnp.where(mask, sp, 0.0) if diagonal else sp).astype(BF16)
            between = jnp.dot(spb, upper, preferred_element_type=F32)
            upto = upto_ref[rows, :] + between[:, 0:1] + spb[:, 0:1].astype(F32)
            upto_ref[rows, :] = upto
            a = jnp.exp2(logsig - between - (s_ref[rows, 0:1] - upto))
            if diagonal:
                a = jnp.where(mask, a, 0.0)
            g = a * lax.dot_general(dob, vb, nt, preferred_element_type=F32)
            hcum = jnp.dot(g.astype(BF16), lower_inc, preferred_element_type=F32) + hsum_ref[rows, :]
            hsum_ref[rows, :] = hcum[:, T - 1:T]
            dz = (g - jnp.exp2(logsig) * hcum) * SB_SCALE
            if diagonal:
                dz = jnp.where(mask, dz, 0.0)
            dzb, ab = dz.astype(BF16), a.astype(BF16)
            dq_ref[rows, :] += jnp.dot(dzb, kb, preferred_element_type=F32)
            dk_ref[pl.ds(start, T), :] += lax.dot_general(dzb, q, tn, preferred_element_type=F32)
            dv_ref[pl.ds(start, T), :] += lax.dot_general(ab, dob, tn, preferred_element_type=F32)

        def step(jb, carry):
            strip(0, M, jb, False)
            return carry

        lax.fori_loop(0, R * i, step, 0)
        for d in range(R):
            strip(d * T, M - d * T, R * i + d, True)

    shp = jax.ShapeDtypeStruct((L, H * HEAD_DIM), F32)
    return pl.pallas_call(
        body, name=name, grid=(H, L // M),
        in_specs=[pl.BlockSpec((M, HEAD_DIM), lambda h, i: (i, h)),
                  pl.BlockSpec((L, HEAD_DIM), lambda h, i: (0, H + h)),
                  pl.BlockSpec((L, HEAD_DIM), lambda h, i: (0, 2 * H + h)),
                  pl.BlockSpec((M, HEAD_DIM), lambda h, i: (i, h)),
                  pl.BlockSpec((M, HEAD_DIM), lambda h, i: (i, h))],
        out_specs=[pl.BlockSpec((M, HEAD_DIM), lambda h, i: (i, h)),
                   pl.BlockSpec((L, HEAD_DIM), lambda h, i: (0, h)),
                   pl.BlockSpec((L, HEAD_DIM), lambda h, i: (0, h))],
        out_shape=[shp, shp, shp],
        scratch_shapes=[pltpu.VMEM((M, HEAD_DIM), BF16), pltpu.VMEM((M, 1), F32), pltpu.VMEM((M, 1), F32)],
        compiler_params=_params("parallel", "arbitrary"),
    )(proj, proj, proj, do, rowsum)


def _head_norm(v):
    r = lax.rsqrt(jnp.mean(v * v, axis=-1, keepdims=True) + EPS)
    return v * r, r


def _memkv_fwd(kv, gk, name):
    M, w2 = kv.shape
    mw = w2 // 2

    def body(kv_ref, gk_ref, k_ref, v_ref):
        for h in range(mw // HEAD_DIM):
            sl = slice(h * HEAD_DIM, (h + 1) * HEAD_DIM)
            xh, _ = _head_norm(kv_ref[:, sl])
            k_ref[:, sl] = xh * gk_ref[...]
        v_ref[...] = kv_ref[:, mw:]

    return pl.pallas_call(body, name=name, out_shape=[jax.ShapeDtypeStruct((M, mw), F32)] * 2,
                          compiler_params=_params())(kv, gk.reshape(1, HEAD_DIM))


def _memkv_bwd(kv, gk, dk, dv, name):
    M, w2 = kv.shape
    mw = w2 // 2

    def body(kv_ref, gk_ref, dk_ref, dv_ref, dkv_ref, dg_ref):
        dg = jnp.zeros((1, HEAD_DIM), F32)
        for h in range(mw // HEAD_DIM):
            sl = slice(h * HEAD_DIM, (h + 1) * HEAD_DIM)
            xh, r = _head_norm(kv_ref[:, sl])
            d = dk_ref[:, sl]
            dg = dg + jnp.sum(d * xh, axis=0, keepdims=True)
            dxh = d * gk_ref[...]
            dkv_ref[:, sl] = r * (dxh - xh * jnp.mean(dxh * xh, axis=-1, keepdims=True))
        dkv_ref[:, mw:] = dv_ref[...]
        dg_ref[...] = dg

    return pl.pallas_call(body, name=name,
                          out_shape=[jax.ShapeDtypeStruct((M, w2), F32), jax.ShapeDtypeStruct((1, HEAD_DIM), F32)],
                          compiler_params=_params())(kv, gk.reshape(1, HEAD_DIM), dk, dv)


def _mem_tile(L):
    return _pick(L, 512, 8)


def _memattn_fwd(proj, khat, v, gq, name):
    L, W = proj.shape
    M, mw = khat.shape
    assert (W - mw) % mw == 0
    qcol = (W - mw) // mw
    tl = _mem_tile(L)
    scale = 1.0 / math.sqrt(HEAD_DIM)

    def body(q_ref, k_ref, v_ref, g_ref, o_ref):
        for h in range(mw // HEAD_DIM):
            sl = slice(h * HEAD_DIM, (h + 1) * HEAD_DIM)
            xh, _ = _head_norm(q_ref[:, sl].astype(F32))
            qh = (xh * g_ref[...]).astype(BF16)
            s = lax.dot_general(qh, k_ref[:, sl].astype(BF16), (((1,), (1,)), ((), ())),
                                preferred_element_type=F32) * scale
            e = jnp.exp(s - jnp.max(s, axis=-1, keepdims=True))
            p = e / jnp.sum(e, axis=-1, keepdims=True)
            o_ref[:, sl] = jnp.dot(p.astype(BF16), v_ref[:, sl].astype(BF16), preferred_element_type=F32)

    whole = lambda a: pl.BlockSpec(a.shape, lambda i: (0,) * a.ndim)
    g2 = gq.reshape(1, HEAD_DIM)
    return pl.pallas_call(
        body, name=name, grid=(L // tl,),
        in_specs=[pl.BlockSpec((tl, mw), lambda i: (i, qcol)), whole(khat), whole(v), whole(g2)],
        out_specs=pl.BlockSpec((tl, mw), lambda i: (i, 0)),
        out_shape=jax.ShapeDtypeStruct((L, mw), F32),
        compiler_params=_params("parallel"),
    )(proj, khat, v, g2)


def _memattn_bwd(proj, khat, v, gq, dcat, name):
    L, W = proj.shape
    M, mw = khat.shape
    qcol = (W - mw) // mw
    dcol = (dcat.shape[1] - mw) // mw
    tl = _mem_tile(L)
    scale = 1.0 / math.sqrt(HEAD_DIM)

    def body(q_ref, k_ref, v_ref, g_ref, do_ref, dq_ref, dk_ref, dv_ref, dg_ref):
        i = pl.program_id(0)

        @pl.when(i == 0)
        def _():
            dk_ref[...] = jnp.zeros_like(dk_ref)
            dv_ref[...] = jnp.zeros_like(dv_ref)
            dg_ref[...] = jnp.zeros_like(dg_ref)

        for h in range(mw // HEAD_DIM):
            sl = slice(h * HEAD_DIM, (h + 1) * HEAD_DIM)
            xh, r = _head_norm(q_ref[:, sl].astype(F32))
            qh = (xh * g_ref[...]).astype(BF16)
            kb = k_ref[:, sl].astype(BF16)
            vb = v_ref[:, sl].astype(BF16)
            dob = do_ref[:, sl].astype(BF16)
            s = lax.dot_general(qh, kb, (((1,), (1,)), ((), ())), preferred_element_type=F32) * scale
            e = jnp.exp(s - jnp.max(s, axis=-1, keepdims=True))
            p = e / jnp.sum(e, axis=-1, keepdims=True)
            dv_ref[:, sl] += lax.dot_general(p.astype(BF16), dob, (((0,), (0,)), ((), ())),
                                             preferred_element_type=F32)
            dp = lax.dot_general(dob, vb, (((1,), (1,)), ((), ())), preferred_element_type=F32)
            ds = (p * (dp - jnp.sum(dp * p, axis=-1, keepdims=True)) * scale).astype(BF16)
            dqh = jnp.dot(ds, kb, preferred_element_type=F32)
            dk_ref[:, sl] += lax.dot_general(ds, qh, (((0,), (0,)), ((), ())), preferred_element_type=F32)
            dg_ref[...] += jnp.sum(dqh * xh, axis=0, keepdims=True)
            dxh = dqh * g_ref[...]
            dq_ref[:, sl] = r * (dxh - xh * jnp.mean(dxh * xh, axis=-1, keepdims=True))

    whole = lambda a: pl.BlockSpec(a.shape, lambda i: (0,) * a.ndim)
    g2 = gq.reshape(1, HEAD_DIM)
    kshape = jax.ShapeDtypeStruct((M, mw), F32)
    return pl.pallas_call(
        body, name=name, grid=(L // tl,),
        in_specs=[pl.BlockSpec((tl, mw), lambda i: (i, qcol)), whole(khat), whole(v), whole(g2),
                  pl.BlockSpec((tl, mw), lambda i: (i, dcol))],
        out_specs=[pl.BlockSpec((tl, mw), lambda i: (i, 0)), whole(kshape), whole(kshape),
                   pl.BlockSpec((1, HEAD_DIM), lambda i: (0, 0))],
        out_shape=[jax.ShapeDtypeStruct((L, mw), F32), kshape, kshape, jax.ShapeDtypeStruct((1, HEAD_DIM), F32)],
        compiler_params=_params("arbitrary"),
    )(proj, khat, v, g2, dcat)


def _cmul(ar, ai, br, bi):
    return ar * br - ai * bi, ar * bi + ai * br


def _s5_discretize(log_dt, a_re, a_im):
    dt = jnp.exp(log_dt)
    mag = jnp.exp(a_re * dt)
    ab_re, ab_im = mag * jnp.cos(a_im * dt), mag * jnp.sin(a_im * dt)
    den = a_re * a_re + a_im * a_im
    inv_re, inv_im = a_re / den, -a_im / den
    f_re, f_im = _cmul(ab_re - 1.0, ab_im, inv_re, inv_im)
    return dt, ab_re, ab_im, inv_re, inv_im, f_re, f_im


def _s5_prep_fwd(log_dt, a_re, a_im, bt_re, bt_im, name):
    G, N = a_re.shape
    C = bt_re.shape[0]

    def body(ld_ref, ar_ref, ai_ref, br_ref, bi_ref, pr_ref, pi_ref, bbr_ref, bbi_ref):
        _, ab_re, ab_im, _, _, f_re, f_im = _s5_discretize(ld_ref[...], ar_ref[...], ai_ref[...])
        p_re, p_im = ab_re, ab_im
        for e in range(8):
            pr_ref[e] = p_re
            pi_ref[e] = p_im
            p_re, p_im = _cmul(p_re, p_im, ab_re, ab_im)
        for ch in range(C):
            bbr_ref[ch], bbi_ref[ch] = _cmul(f_re, f_im, br_ref[ch], bi_ref[ch])

    pw = jax.ShapeDtypeStruct((8, G, N), F32)
    bb = jax.ShapeDtypeStruct((C, G, N), F32)
    return pl.pallas_call(body, name=name, out_shape=[pw, pw, bb, bb], compiler_params=_params())(
        log_dt.reshape(G, 1), a_re, a_im, bt_re, bt_im)


def _s5_prep_bwd(log_dt, a_re, a_im, bt_re, bt_im, ga_re, ga_im, gbb_re, gbb_im, name):
    G, N = a_re.shape
    C = bt_re.shape[0]

    def body(ld_ref, ar_ref, ai_ref, br_ref, bi_ref, gar_ref, gai_ref, gbr_ref, gbi_ref,
             gld_ref, gare_ref, gaim_ref, gbtr_ref, gbti_ref):
        a_re_, a_im_ = ar_ref[...], ai_ref[...]
        dt, ab_re, ab_im, inv_re, inv_im, f_re, f_im = _s5_discretize(ld_ref[...], a_re_, a_im_)
        gf_re, gf_im = jnp.zeros((G, N), F32), jnp.zeros((G, N), F32)
        for ch in range(C):
            gbr, gbi = gbr_ref[ch], gbi_ref[ch]
            gbtr_ref[ch], gbti_ref[ch] = _cmul(f_re, -f_im, gbr, gbi)
            tr, ti = _cmul(br_ref[ch], -bi_ref[ch], gbr, gbi)
            gf_re, gf_im = gf_re + tr, gf_im + ti
        t_re, t_im = _cmul(gf_re, gf_im, inv_re, -inv_im)
        gab_re, gab_im = gar_ref[...] + t_re, gai_ref[...] + t_im
        u_re, u_im = _cmul(dt * ab_re, -dt * ab_im, gab_re, gab_im)
        fl_re, fl_im = _cmul(f_re, f_im, inv_re, inv_im)
        w_re, w_im = _cmul(fl_re, -fl_im, gf_re, gf_im)
        gare_ref[...] = u_re - w_re
        gaim_ref[...] = u_im - w_im
        la_re, la_im = _cmul(a_re_, a_im_, ab_re, ab_im)
        gdt = jnp.sum(la_re * gab_re + la_im * gab_im, axis=1, keepdims=True)
        gld_ref[...] = dt * gdt

    gn = jax.ShapeDtypeStruct((G, N), F32)
    bb = jax.ShapeDtypeStruct((C, G, N), F32)
    return pl.pallas_call(body, name=name,
                          out_shape=[jax.ShapeDtypeStruct((G, 1), F32), gn, gn, bb, bb],
                          compiler_params=_params())(
        log_dt.reshape(G, 1), a_re, a_im, bt_re, bt_im, ga_re, ga_im, gbb_re, gbb_im)


def _dot3(a, b, dims=(((1,), (0,)), ((), ()))):
    ah, al = _split_bf16(a)
    bh, bl = _split_bf16(b)
    d = lambda p, q: lax.dot_general(p, q, dims, preferred_element_type=F32)
    return d(ah, bh) + d(ah, bl) + d(al, bh)


def _block_diag(t):
    nch, P, r, c = t.shape
    eye = jnp.eye(P, dtype=t.dtype)
    return (t[:, :, :, None, :] * eye[None, :, None, :, None]).reshape(nch, P * r, P * c)


def _block_diag_extract(m, P):
    nch, R, Cc = m.shape
    r, c = R // P, Cc // P
    m5 = m.reshape(nch, P, r, P, c)
    return jnp.stack([m5[:, k, :, k, :] for k in range(P)], axis=1)


def _s5_tables(pw_re, pw_im):
    pr = pw_re.reshape(8, -1)
    pi = pw_im.reshape(8, -1)
    row = jnp.arange(8)[:, None]
    tiles = []
    for d in (1, 2, 4):
        keep = row >= d
        tiles += [jnp.where(keep, pr[d - 1][None], 0.0), jnp.where(keep, pi[d - 1][None], 0.0)]
    tiles += [pr, pi]
    for d in (1, 2, 4):
        keep = row <= 7 - d
        tiles += [jnp.where(keep, pr[d - 1][None], 0.0), jnp.where(keep, -pi[d - 1][None], 0.0)]
    tiles += [pr[::-1], -pi[::-1]]
    return jnp.stack(tiles, axis=0)


def _scan_tile(sr, si, m, shifts):
    for n, sh in enumerate(shifts):
        rr, ri = pltpu.roll(sr, sh, 0), pltpu.roll(si, sh, 0)
        mr, mi = m[2 * n], m[2 * n + 1]
        sr, si = sr + mr * rr - mi * ri, si + mr * ri + mi * rr
    return sr, si


def _s5_rows(L):
    return _pick(L, 512, 8)


def _s5_fwd(proj, bd_re, bd_im, cd_re, cd_imn, tables, d_skip, tok_w, name):
    L = proj.shape[0]
    nch = tok_w // LANE
    ns = S5_PACK * S5_STATE
    tl = _s5_rows(L)
    nr = L // tl

    def body(u_ref, bdr_ref, bdi_ref, cdr_ref, cdi_ref, tab_ref, d_ref, y_ref, xr_ref, xi_ref, carry):
        r = pl.program_id(1)

        @pl.when(r == 0)
        def _():
            carry[...] = jnp.zeros_like(carry)

        u = u_ref[...]
        xr_ref[...] = _dot3(u, bdr_ref[0])
        xi_ref[...] = _dot3(u, bdi_ref[0])
        m = [tab_ref[n] for n in range(8)]

        def step(t, c):
            cr, ci = c
            rows = pl.ds(pl.multiple_of(t * 8, 8), 8)
            sr, si = _scan_tile(xr_ref[rows, :], xi_ref[rows, :], m, (1, 2, 4))
            sr, si = sr + m[6] * cr - m[7] * ci, si + m[6] * ci + m[7] * cr
            xr_ref[rows, :] = sr
            xi_ref[rows, :] = si
            return jnp.broadcast_to(sr[7:8, :], (8, ns)), jnp.broadcast_to(si[7:8, :], (8, ns))

        cr, ci = lax.fori_loop(0, tl // 8, step, (carry[0], carry[1]))
        carry[0] = cr
        carry[1] = ci
        y_ref[...] = _dot3(xr_ref[...], cdr_ref[0]) + _dot3(xi_ref[...], cdi_ref[0]) + d_ref[...] * u

    chunk3 = lambda a: pl.BlockSpec((1,) + a.shape[1:], lambda j, r: (j, 0, 0))
    return pl.pallas_call(
        body, name=name, grid=(nch, nr),
        in_specs=[pl.BlockSpec((tl, LANE), lambda j, r: (r, j)),
                  chunk3(bd_re), chunk3(bd_im), chunk3(cd_re), chunk3(cd_imn),
                  pl.BlockSpec((8, 8, ns), lambda j, r: (0, 0, j)),
                  pl.BlockSpec((1, LANE), lambda j, r: (0, j))],
        out_specs=[pl.BlockSpec((tl, LANE), lambda j, r: (r, j)),
                   pl.BlockSpec((tl, ns), lambda j, r: (r, j)),
                   pl.BlockSpec((tl, ns), lambda j, r: (r, j))],
        out_shape=[jax.ShapeDtypeStruct((L, tok_w), F32),
                   jax.ShapeDtypeStruct((L, nch * ns), F32), jax.ShapeDtypeStruct((L, nch * ns), F32)],
        scratch_shapes=[pltpu.VMEM((2, 8, ns), F32)],
        compiler_params=_params("parallel", "arbitrary"),
    )(proj, bd_re, bd_im, cd_re, cd_imn, tables, d_skip.reshape(1, tok_w))


def _s5_bwd(proj, dy, xr, xi, bd_re, bd_im, cd_re, cd_imn, tables, d_skip, tok_w, name):
    L = proj.shape[0]
    nch = tok_w // LANE
    ns = S5_PACK * S5_STATE
    tl = _s5_rows(L)
    nr = L // tl
    tn = (((0,), (0,)), ((), ()))
    nt = (((1,), (1,)), ((), ()))

    def body(u_ref, dy_ref, xr_ref, xi_ref, bdr_ref, bdi_ref, cdr_ref, cdi_ref, tab_ref, d_ref,
             du_ref, gcr_ref, gci_ref, gbr_ref, gbi_ref, gar_ref, gai_ref, gd_ref, lr, li, carry):
        r = pl.program_id(1)

        @pl.when(r == 0)
        def _():
            carry[...] = jnp.zeros_like(carry)
            for ref in (gcr_ref, gci_ref, gbr_ref, gbi_ref, gar_ref, gai_ref, gd_ref):
                ref[...] = jnp.zeros_like(ref)

        u, dyb = u_ref[...], dy_ref[...]
        lr[...] = _dot3(dyb, cdr_ref[0], nt)
        li[...] = _dot3(dyb, cdi_ref[0], nt)
        m = [tab_ref[n] for n in range(8)]
        last = lax.broadcasted_iota(jnp.int32, (8, ns), 0) == 7

        def step(it, c):
            cr, ci, ar, ai = c
            t = tl // 8 - 1 - it
            rows = pl.ds(pl.multiple_of(t * 8, 8), 8)
            sr, si = _scan_tile(lr[rows, :], li[rows, :], m, (7, 6, 4))
            sr, si = sr + m[6] * cr - m[7] * ci, si + m[6] * ci + m[7] * cr
            lr[rows, :] = sr
            li[rows, :] = si
            nr_, ni_ = jnp.where(last, cr, pltpu.roll(sr, 7, 0)), jnp.where(last, ci, pltpu.roll(si, 7, 0))
            xr_t, xi_t = xr_ref[rows, :], xi_ref[rows, :]
            ar = ar + xr_t * nr_ + xi_t * ni_
            ai = ai + xr_t * ni_ - xi_t * nr_
            return jnp.broadcast_to(sr[0:1, :], (8, ns)), jnp.broadcast_to(si[0:1, :], (8, ns)), ar, ai

        z = jnp.zeros((8, ns), F32)
        cr, ci, ar, ai = lax.fori_loop(0, tl // 8, step, (carry[0], carry[1], z, z))
        carry[0] = cr
        carry[1] = ci
        gar_ref[...] += jnp.sum(ar, axis=0, keepdims=True)
        gai_ref[...] += jnp.sum(ai, axis=0, keepdims=True)
        lam_r, lam_i = lr[...], li[...]
        du_ref[...] = _dot3(lam_r, bdr_ref[0], nt) + _dot3(lam_i, bdi_ref[0], nt) + d_ref[...] * dyb
        gd_ref[...] += jnp.sum(dyb * u, axis=0, keepdims=True)
        gcr_ref[0] += _dot3(xr_ref[...], dyb, tn)
        gci_ref[0] += _dot3(xi_ref[...], dyb, tn)
        gbr_ref[0] += _dot3(u, lam_r, tn)
        gbi_ref[0] += _dot3(u, lam_i, tn)

    rev = lambda j, r: (nr - 1 - r, j)
    chunk3 = lambda a: pl.BlockSpec((1,) + a.shape[1:], lambda j, r: (j, 0, 0))
    gcd = jax.ShapeDtypeStruct((nch, ns, LANE), F32)
    gbd = jax.ShapeDtypeStruct((nch, LANE, ns), F32)
    gab = jax.ShapeDtypeStruct((1, nch * ns), F32)
    return pl.pallas_call(
        body, name=name, grid=(nch, nr),
        in_specs=[pl.BlockSpec((tl, LANE), rev), pl.BlockSpec((tl, LANE), rev),
                  pl.BlockSpec((tl, ns), rev), pl.BlockSpec((tl, ns), rev),
                  chunk3(bd_re), chunk3(bd_im), chunk3(cd_re), chunk3(cd_imn),
                  pl.BlockSpec((8, 8, ns), lambda j, r: (1, 0, j)),
                  pl.BlockSpec((1, LANE), lambda j, r: (0, j))],
        out_specs=[pl.BlockSpec((tl, LANE), rev), chunk3(gcd), chunk3(gcd), chunk3(gbd), chunk3(gbd),
                   pl.BlockSpec((1, ns), lambda j, r: (0, j)), pl.BlockSpec((1, ns), lambda j, r: (0, j)),
                   pl.BlockSpec((1, LANE), lambda j, r: (0, j))],
        out_shape=[jax.ShapeDtypeStruct((L, tok_w), F32), gcd, gcd, gbd, gbd, gab, gab,
                   jax.ShapeDtypeStruct((1, tok_w), F32)],
        scratch_shapes=[pltpu.VMEM((tl, ns), F32), pltpu.VMEM((tl, ns), F32), pltpu.VMEM((2, 8, ns), F32)],
        compiler_params=_params("parallel", "arbitrary"),
    )(proj, dy, xr, xi, bd_re, bd_im, cd_re, cd_imn, tables, d_skip.reshape(1, tok_w))


def _lin(dev):
    return 4 * dev[0] + 2 * dev[1] + dev[2]


def _shard_view(ref, axis, index, size):
    ix = tuple(pl.ds(index * size, size) if d == axis else pl.ds(0, ref.shape[d]) for d in range(len(ref.shape)))
    return ref.at[ix]


def _slot_index(dev, paired):
    idx = _lin(dev)
    half = N_DEV // 2
    return 2 * (idx % half) + idx // half if paired else idx


def _comm_sems(n):
    return [pltpu.SemaphoreType.DMA((n, 7)), pltpu.SemaphoreType.DMA((n, 7)), pltpu.SemaphoreType.DMA((n,))]


def _gather_side(shards, axes, paired=None):
    n = len(shards)
    paired = [False] * n if paired is None else paired
    sizes = [s.shape[a] for s, a in zip(shards, axes)]

    def tools(ins, outs, sems):
        send_sems, recv_sems, local_sems = sems
        x, y, c = lax.axis_index("x"), lax.axis_index("y"), lax.axis_index("c")
        me, sibling = (x, y, c), (x, y, 1 - c)
        chips = [(1 - x, y), (x, 1 - y), (1 - x, 1 - y)]

        def slot(a, dev):
            return _shard_view(outs[a], axes[a], _slot_index(dev, paired[a]), sizes[a])

        def copy(a, k, block, to, src=None):
            return pltpu.make_async_remote_copy(
                src_ref=slot(a, block) if src is None else src, dst_ref=slot(a, block),
                send_sem=send_sems.at[a, k], recv_sem=recv_sems.at[a, k],
                device_id=to, device_id_type=MESH)

        mine = [pltpu.make_async_copy(ins[a], slot(a, me), local_sems.at[a]) for a in range(n)]
        first = []
        for a in range(n):
            first.append(copy(a, 0, me, sibling, src=ins[a]))
            first += [copy(a, 1 + j, me, (*chip, c), src=ins[a]) for j, chip in enumerate(chips)]
        return me, sibling, chips, c, copy, mine, first

    def start(ins, outs, sems):
        _, _, _, _, _, mine, first = tools(ins, outs, sems)
        for cp in mine + first:
            cp.start()

    def finish(ins, outs, sems):
        me, sibling, chips, c, copy, mine, first = tools(ins, outs, sems)
        passed = []
        for j, chip in enumerate(chips):
            for a in range(n):
                copy(a, 1 + j, (*chip, c), me).wait_recv()
                cp = copy(a, 4 + j, (*chip, c), sibling)
                cp.start()
                passed.append(cp)
        for a in range(n):
            copy(a, 0, sibling, me).wait_recv()
            for j, chip in enumerate(chips):
                copy(a, 4 + j, (*chip, 1 - c), me).wait_recv()
        for cp in first + passed:
            cp.wait_send()
        for cp in mine:
            cp.wait()

    out_shape = []
    for s, a in zip(shards, axes):
        shp = list(s.shape)
        shp[a] *= N_DEV
        out_shape.append(jax.ShapeDtypeStruct(tuple(shp), s.dtype))
    return _Side(list(shards), out_shape, _comm_sems(n), start, finish)


def _scatter_side(fulls, axes, paired=None):
    n = len(fulls)
    paired = [False] * n if paired is None else paired
    sizes = [f.shape[a] // N_DEV for f, a in zip(fulls, axes)]
    masks = [(mx, my, mc) for mx in (0, 1) for my in (0, 1) for mc in (0, 1)][1:]

    def copies(ins, outs, sems):
        send_sems, recv_sems, local_sems = sems
        x, y, c = lax.axis_index("x"), lax.axis_index("y"), lax.axis_index("c")
        me = (x, y, c)
        flip = lambda v, m: 1 - v if m else v
        piece = lambda a, dev: _shard_view(ins[a], axes[a], _slot_index(dev, paired[a]), sizes[a])
        local = [pltpu.make_async_copy(piece(a, me), outs[a].at[_lin(me)], local_sems.at[a]) for a in range(n)]
        sends = []
        for k, (mx, my, mc) in enumerate(masks):
            peer = (flip(x, mx), flip(y, my), flip(c, mc))
            for a in range(n):
                sends.append(pltpu.make_async_remote_copy(
                    src_ref=piece(a, peer), dst_ref=outs[a].at[_lin(me)],
                    send_sem=send_sems.at[a, k], recv_sem=recv_sems.at[a, k],
                    device_id=peer, device_id_type=MESH))
        return local, sends

    def start(ins, outs, sems):
        local, sends = copies(ins, outs, sems)
        for cp in local + sends:
            cp.start()

    def finish(ins, outs, sems):
        local, sends = copies(ins, outs, sems)
        for cp in sends:
            cp.wait()
        for cp in local:
            cp.wait()

    out_shape = []
    for f, a, sz in zip(fulls, axes, sizes):
        shp = list(f.shape)
        shp[a] = sz
        out_shape.append(jax.ShapeDtypeStruct((N_DEV,) + tuple(shp), f.dtype))
    return _Side(list(fulls), out_shape, _comm_sems(n), start, finish)


def _all_gather(shards, axes, name, paired=None):
    return _run_side(_gather_side(shards, axes, paired), name)


def _adamw_sum(w, m, v, parts, name):
    rows, cols = w.shape
    c1 = 1.0 / (1.0 - ADAM_B1 ** ADAM_STEP)
    c2 = 1.0 / (1.0 - ADAM_B2 ** ADAM_STEP)

    def fn(j, wb, mb, vb, *ps):
        g = ps[0].astype(F32)
        for p in ps[1:]:
            g = g + p.astype(F32)
        mn = ADAM_B1 * mb + (1.0 - ADAM_B1) * g
        vn = ADAM_B2 * vb + (1.0 - ADAM_B2) * (g * g)
        delta = -ADAM_LR * ((mn * c1) / (jnp.sqrt(vn * c2) + ADAM_EPS) + ADAM_WD * wb)
        return g, delta, mn, vn

    per_row = cols * (12 + 16 + N_DEV * parts.dtype.itemsize)
    tl = _row_tile(rows, per_row, budget=12 * 1024 * 1024)
    flat = parts.reshape(N_DEV * rows, cols)
    nblk = rows // tl
    ins = [w, m, v] + [(flat, d * nblk, 0) for d in range(N_DEV)]
    return _ew(fn, ins, [(cols, F32)] * 4, rows=rows, cols=cols, tl=tl, name=name)


_NAMES = ['ffn1_norm', 'ffn1_w_gu', 'ffn1_w_down', 'mix_norm', 'mem_norm', 'w_mem_kv', 'xq_norm', 'xk_norm',
          'w_out', 'ffn2_norm', 'ffn2_w_gu', 'ffn2_w_down', 'sb_w_in', 's5_w_in', 's5_log_dt', 's5_a_re',
          's5_a_im', 's5_b_re', 's5_b_im', 's5_c_re', 's5_c_im', 's5_d', 's5_w_glu']
_SHARDED = {'ffn1_w_gu': 1, 'ffn1_w_down': 0, 'w_mem_kv': 0, 'w_out': 0, 'ffn2_w_gu': 1, 'ffn2_w_down': 0,
            'sb_w_in': 1, 's5_w_in': 0, 's5_w_glu': 0}
_PER_LAYER = ['ffn1_w_gu', 'ffn1_w_down', 'w_mem_kv', 'w_out', 'ffn2_w_gu', 'ffn2_w_down']
_SMALL = ['ffn1_norm', 'mix_norm', 'mem_norm', 'xq_norm', 'xk_norm', 'ffn2_norm', 's5_log_dt', 's5_a_re',
          's5_a_im', 's5_b_re', 's5_b_im', 's5_c_re', 's5_c_im']


def _ffn_fwd(x, gain, w_gu, w_down, tag, side=None):
    h = _rmsnorm_fwd(x, gain, f"{tag}_norm")
    gu, s, extra = _mm_swiglu(h, w_gu, f"{tag}_gu", side)
    out = _mm(s, w_down, res=x, scale=0.5, name=f"{tag}_down")
    return out, (x, h, gu, s), extra


def _ffn_bwd(dout, dout16, saved, gain, w_gu, w_down, tag):
    x, h, gu, s = saved
    g_down = _mm(s, dout16, ta=True, scale=0.5, out_dtype=BF16, name=f"{tag}_gdown")
    dgu, (p_down,) = _mm_dswiglu(dout16, w_down, gu, f"{tag}_ds", _scatter_side([g_down], [0]))
    g_gu = _mm(h, dgu, ta=True, out_dtype=BF16, name=f"{tag}_ggu")
    dh, (p_gu,) = _mm(dgu, w_gu, tb=True, name=f"{tag}_dh", side=_scatter_side([g_gu], [1], [True]))
    dx, dx16, g_gain = _rmsnorm_bwd(dh, x, gain, dout, f"{tag}_dnorm")
    return dx, dx16, g_gain, p_gu, p_down


def _mem_fwd(mem, mem_gain, w_kv, gk, tag):
    mem_h = _rmsnorm_fwd(mem, mem_gain, f"{tag}_memnorm")
    kv = _mm(mem_h, w_kv, name=f"{tag}_memkv")
    khat, v = _memkv_fwd(kv, gk, f"{tag}_memk")
    return mem_h, kv, khat, v


def _s5_arrange(p, j, nch):
    bt_re = jnp.transpose(p['s5_b_re'][j], (2, 0, 1))
    bt_im = jnp.transpose(p['s5_b_im'][j], (2, 0, 1))
    return bt_re, bt_im


def _mixer_fwd(i, x, p, W, mem, tag, side=None):
    L, D = x.shape
    mw = D // 4
    tok_w = D - mw
    j = i // 2
    h = _rmsnorm_fwd(x, p['mix_norm'][i], f"{tag}_norm")
    mem_h, kv, khat, v = _mem_fwd(mem, p['mem_norm'][i], W['w_mem_kv'], p['xk_norm'][i], tag)
    sv = dict(x=x, h=h, mem_h=mem_h, kv=kv, khat=khat, v=v)
    proj = _mm(h, W['sb_w_in' if i % 2 == 0 else 's5_w_in'], out_dtype=BF16 if i % 2 == 0 else F32,
               name=f"{tag}_in", side=side)
    extra = ()
    if side is not None:
        proj, extra = proj
    if i % 2 == 0:
        tok, rowsum = _sb_fwd(proj, tok_w // HEAD_DIM, f"{tag}_sb")
        sv.update(rowsum=rowsum)
    else:
        nch = tok_w // LANE
        bt_re, bt_im = _s5_arrange(p, j, nch)
        pw_re, pw_im, bb_re, bb_im = _s5_prep_fwd(p['s5_log_dt'][j], p['s5_a_re'][j], p['s5_a_im'][j],
                                                  bt_re, bt_im, f"{tag}_s5prep")
        C, N = bb_re.shape[0], bb_re.shape[2]
        bd_re = _block_diag(jnp.swapaxes(bb_re, 0, 1).reshape(nch, S5_PACK, C, N))
        bd_im = _block_diag(jnp.swapaxes(bb_im, 0, 1).reshape(nch, S5_PACK, C, N))
        ct_re = jnp.swapaxes(p['s5_c_re'][j], 1, 2).reshape(nch, S5_PACK, N, C)
        ct_im = jnp.swapaxes(p['s5_c_im'][j], 1, 2).reshape(nch, S5_PACK, N, C)
        cd_re, cd_imn = _block_diag(ct_re), _block_diag(-ct_im)
        tables = _s5_tables(pw_re, pw_im)
        y, xr, xi = _s5_fwd(proj, bd_re, bd_im, cd_re, cd_imn, tables, W['s5_d'][j], tok_w, f"{tag}_s5")
        tl = _row_tile(L, tok_w * 12)
        yg = _ew(lambda jj, yb: (_gelu(yb),), [y], [(tok_w, BF16)], rows=L, cols=tok_w, tl=tl, name=f"{tag}_gelu")[0]
        pre = _mm(yg, W['s5_w_glu'], name=f"{tag}_glu")
        tok = _ew(lambda jj, yb, pb: (_gelu(yb) * jax.nn.sigmoid(pb),), [y, pre], [(tok_w, F32)],
                  rows=L, cols=tok_w, tl=tl, name=f"{tag}_gate")[0]
        sv.update(y=y, xr=xr, xi=xi, pre=pre, yg=yg, bt_re=bt_re, bt_im=bt_im, bd_re=bd_re, bd_im=bd_im,
                  cd_re=cd_re, cd_imn=cd_imn, tables=tables)
    cross = _memattn_fwd(proj, khat, v, p['xq_norm'][i], f"{tag}_mem")
    cat = jnp.concatenate([tok.astype(BF16), cross.astype(BF16)], axis=1)
    out = _mm(cat, W['w_out'], res=x, name=f"{tag}_out")
    sv.update(proj=proj, cat=cat)
    return out, sv, extra


def _mixer_bwd(i, dout, dout16, sv, p, W, mem, tag):
    x, h, proj, cat = sv['x'], sv['h'], sv['proj'], sv['cat']
    L, D = x.shape
    mw = D // 4
    tok_w = D - mw
    j = i // 2
    g, parts = {}, {}
    g_out = _mm(cat, dout16, ta=True, out_dtype=BF16, name=f"{tag}_gout")
    dcat, (parts['w_out'],) = _mm(dout16, W['w_out'], tb=True, name=f"{tag}_dcat",
                                  side=_scatter_side([g_out], [_SHARDED['w_out']]))
    dqm, dkhat, dv, g_xq = _memattn_bwd(proj, sv['khat'], sv['v'], p['xq_norm'][i], dcat, f"{tag}_dmem")
    g['xq_norm'] = g_xq
    dkv, g_xk = _memkv_bwd(sv['kv'], p['xk_norm'][i], dkhat, dv, f"{tag}_dmemk")
    g['xk_norm'] = g_xk
    late = {'w_mem_kv': _mm(sv['mem_h'], dkv, ta=True, out_dtype=BF16, name=f"{tag}_gmemkv")}
    dmem_h = _mm(dkv, W['w_mem_kv'], tb=True, name=f"{tag}_dmemh")
    _, _, g['mem_norm'] = _rmsnorm_bwd(dmem_h, mem, p['mem_norm'][i], None, f"{tag}_dmemnorm")
    if i % 2 == 0:
        dq, dk, dvv = _sb_bwd(proj, dcat, sv['rowsum'], tok_w // HEAD_DIM, f"{tag}_dsb")
        dproj = jnp.concatenate([dq.astype(BF16), dk.astype(BF16), dvv.astype(BF16), dqm.astype(BF16)], axis=1)
        w_in = W['sb_w_in']
        late['sb_w_in'] = _mm(h, dproj, ta=True, out_dtype=BF16, name=f"{tag}_gin")
    else:
        y, pre, yg = sv['y'], sv['pre'], sv['yg']
        tl = _row_tile(L, tok_w * 20)

        def gate_bwd(jj, dt, yb, pb):
            gl = _gelu(yb)
            sg = jax.nn.sigmoid(pb)
            return dt * gl * sg * (1.0 - sg)

        dpre = _ew(gate_bwd, [dcat, y, pre], [(tok_w, BF16)], rows=L, cols=tok_w, tl=tl, name=f"{tag}_dgate")[0]
        late['s5_w_glu'] = _mm(yg, dpre, ta=True, out_dtype=BF16, name=f"{tag}_gglu")
        dyg = _mm(dpre, W['s5_w_glu'], tb=True, name=f"{tag}_dyg")

        def gelu_bwd(jj, dt, yb, pb, db):
            return (dt * jax.nn.sigmoid(pb) + db) * _gelu_grad(yb)

        dy = _ew(gelu_bwd, [dcat, y, pre, dyg], [(tok_w, F32)], rows=L, cols=tok_w, tl=tl, name=f"{tag}_dgelu")[0]
        du, gcd_re, gcd_imn, gbd_re, gbd_im, ga_re, ga_im, g_d = _s5_bwd(
            proj, dy, sv['xr'], sv['xi'], sv['bd_re'], sv['bd_im'], sv['cd_re'], sv['cd_imn'], sv['tables'],
            W['s5_d'][j], tok_w, f"{tag}_ds5")
        G, N = p['s5_a_re'].shape[1], p['s5_a_re'].shape[2]
        C = S5_GROUP
        gct_re = _block_diag_extract(gcd_re, S5_PACK).reshape(G, N, C)
        gct_im = -_block_diag_extract(gcd_imn, S5_PACK).reshape(G, N, C)
        g['s5_c_re'] = jnp.swapaxes(gct_re, 1, 2)
        g['s5_c_im'] = jnp.swapaxes(gct_im, 1, 2)
        gbb_re = jnp.swapaxes(_block_diag_extract(gbd_re, S5_PACK).reshape(G, C, N), 0, 1)
        gbb_im = jnp.swapaxes(_block_diag_extract(gbd_im, S5_PACK).reshape(G, C, N), 0, 1)
        g_ld, g_are, g_aim, gbt_re, gbt_im = _s5_prep_bwd(
            p['s5_log_dt'][j], p['s5_a_re'][j], p['s5_a_im'][j], sv['bt_re'], sv['bt_im'],
            ga_re.reshape(G, N), ga_im.reshape(G, N), gbb_re, gbb_im, f"{tag}_ds5prep")
        g['s5_log_dt'] = g_ld.reshape(G)
        g['s5_a_re'], g['s5_a_im'] = g_are, g_aim
        g['s5_b_re'], g['s5_b_im'] = jnp.transpose(gbt_re, (1, 2, 0)), jnp.transpose(gbt_im, (1, 2, 0))
        g['s5_d'] = g_d.reshape(tok_w)
        dproj = jnp.concatenate([du.astype(BF16), dqm.astype(BF16)], axis=1)
        w_in = W['s5_w_in']
        late['s5_w_in'] = _mm(h, dproj, ta=True, out_dtype=BF16, name=f"{tag}_gin")
    names = list(late)
    dh, bufs = _mm(dproj, w_in, tb=True, name=f"{tag}_dh",
                   side=_scatter_side([late[n] for n in names], [_SHARDED[n] for n in names]))
    parts.update(zip(names, bufs))
    dx, dx16, g['mix_norm'] = _rmsnorm_bwd(dh, x, p['mix_norm'][i], dout, f"{tag}_dnorm")
    return dx, dx16, g, parts


def _pack(arrs):
    flat = jnp.concatenate([a.reshape(-1).astype(F32) for a in arrs])
    pad = (-flat.shape[0]) % (512 * LANE)
    flat = jnp.pad(flat, (0, pad))
    return flat.reshape(-1, LANE)


def _unpack(flat, like):
    flat = flat.reshape(-1)
    out, off = [], 0
    for a in like:
        n = math.prod(a.shape)
        out.append(flat[off:off + n].reshape(a.shape))
        off += n
    return out


def kernel(x, mem, ffn1_norm, ffn1_w_gu, ffn1_w_down, mix_norm, mem_norm, w_mem_kv, xq_norm, xk_norm, w_out, ffn2_norm, ffn2_w_gu, ffn2_w_down, sb_w_in, s5_w_in, s5_log_dt, s5_a_re, s5_a_im, s5_b_re, s5_b_im, s5_c_re, s5_c_im, s5_d, s5_w_glu, loss_target, m_ffn1_norm, m_ffn1_w_gu, m_ffn1_w_down, m_mix_norm, m_mem_norm, m_w_mem_kv, m_xq_norm, m_xk_norm, m_w_out, m_ffn2_norm, m_ffn2_w_gu, m_ffn2_w_down, m_sb_w_in, m_s5_w_in, m_s5_log_dt, m_s5_a_re, m_s5_a_im, m_s5_b_re, m_s5_b_im, m_s5_c_re, m_s5_c_im, m_s5_d, m_s5_w_glu, v_ffn1_norm, v_ffn1_w_gu, v_ffn1_w_down, v_mix_norm, v_mem_norm, v_w_mem_kv, v_xq_norm, v_xk_norm, v_w_out, v_ffn2_norm, v_ffn2_w_gu, v_ffn2_w_down, v_sb_w_in, v_s5_w_in, v_s5_log_dt, v_s5_a_re, v_s5_a_im, v_s5_b_re, v_s5_b_im, v_s5_c_re, v_s5_c_im, v_s5_d, v_s5_w_glu):
    given = dict(locals())
    p = {n: given[n] for n in _NAMES}
    mom = {n: given["m_" + n] for n in _NAMES}
    var = {n: given["v_" + n] for n in _NAMES}
    depth = ffn1_norm.shape[0]
    xs, mems, target = x[0], mem[0], loss_target[0]
    L, D = xs.shape
    me = 4 * lax.axis_index("x") + 2 * lax.axis_index("y") + lax.axis_index("c")

    def layer_slice(name, i):
        return p[name][i // 2] if name in ('sb_w_in', 's5_w_in', 's5_w_glu') else p[name][i]

    def layer_names(i):
        return _PER_LAYER + (['sb_w_in'] if i % 2 == 0 else ['s5_w_in', 's5_w_glu'])

    n_b = s5_d.shape[0]
    d_rows = jnp.broadcast_to(s5_d.reshape(1, -1), (8, s5_d.size))
    d_full = _all_gather([d_rows], [0], "gather_s5d")[0][::8]
    d_full = jnp.swapaxes(d_full.reshape(N_DEV, n_b, -1), 0, 1).reshape(n_b, -1)

    def gather_group(i, names):
        shards = [_cast_bf16(layer_slice(nm, i), f"cast_{nm}") for nm in names]
        return _gather_side(shards, [_SHARDED[nm] for nm in names], [nm.endswith('w_gu') for nm in names])

    groups = lambda i: (['ffn1_w_gu', 'ffn1_w_down'], [nm for nm in layer_names(i) if 'ffn' not in nm],
                        ['ffn2_w_gu', 'ffn2_w_down'])
    W = {}
    for names in groups(0):
        W.update(zip(names, _run_side(gather_group(0, names), f"gather_first_{names[0]}")))
    Ws, saves = [], []
    act = xs
    for i in range(depth):
        W['s5_d'] = d_full
        Ws.append(W)
        nxt = [gather_group(i + 1, names) if i + 1 < depth else None for names in groups(i + 1)]
        act, s1, e1 = _ffn_fwd(act, p['ffn1_norm'][i], W['ffn1_w_gu'], W['ffn1_w_down'], "ffn1", nxt[0])
        act, s2, e2 = _mixer_fwd(i, act, p, W, mems, f"mix{i % 2}", nxt[1])
        act, s3, e3 = _ffn_fwd(act, p['ffn2_norm'][i], W['ffn2_w_gu'], W['ffn2_w_down'], "ffn2", nxt[2])
        saves.append((s1, s2, s3))
        W = {}
        if i + 1 < depth:
            for names, got in zip(groups(i + 1), (e1, e2, e3)):
                W.update(zip(names, got))

    loss_part, dact, dact16 = _loss_fwd_bwd(act, target, "loss")
    loss = lax.psum(loss_part[0, 0], ("x", "y", "c"))

    small_g = {n: [None] * p[n].shape[0] for n in _SMALL + ['s5_d']}
    big = {n: [None] * p[n].shape[0] for n in _SHARDED}
    for i in reversed(range(depth)):
        W = Ws[i]
        s1, s2, s3 = saves[i]
        j = i // 2
        parts = {}
        dact, dact16, gg, parts['ffn2_w_gu'], parts['ffn2_w_down'] = _ffn_bwd(
            dact, dact16, s3, p['ffn2_norm'][i], W['ffn2_w_gu'], W['ffn2_w_down'], "ffn2")
        small_g['ffn2_norm'][i] = gg
        dact, dact16, gm, pm = _mixer_bwd(i, dact, dact16, s2, p, W, mems, f"mix{i % 2}")
        parts.update(pm)
        dact, dact16, gg, parts['ffn1_w_gu'], parts['ffn1_w_down'] = _ffn_bwd(
            dact, dact16, s1, p['ffn1_norm'][i], W['ffn1_w_gu'], W['ffn1_w_down'], "ffn1")
        small_g['ffn1_norm'][i] = gg
        for n in _SMALL + ['s5_d']:
            if n in gm:
                small_g[n][j if n.startswith('s5_') else i] = gm[n]
        for nm in layer_names(i):
            li = j if nm in ('sb_w_in', 's5_w_in', 's5_w_glu') else i
            big[nm][li] = _adamw_sum(p[nm][li], mom[nm][li], var[nm][li], parts[nm], f"adamw_{nm}")

    small_names = _SMALL + ['s5_d']
    g_small = [jnp.stack([gi.reshape(p[n].shape[1:]) if n != 's5_d' else gi for gi in small_g[n]])
               for n in small_names]
    d_w = jnp.zeros_like(d_full)
    cw = s5_d.shape[1]
    w_small = [p[n] for n in _SMALL]
    m_small = [mom[n] for n in _SMALL]
    v_small = [var[n] for n in _SMALL]
    place = lambda a: lax.dynamic_update_slice(d_w, a, (0, me * cw))
    flat_g = _pack(g_small)
    flat_w = _pack(w_small + [place(s5_d)])
    flat_m = _pack(m_small + [place(mom['s5_d'])])
    flat_v = _pack(v_small + [place(var['s5_d'])])
    parts = _all_gather([flat_g.reshape((1,) + flat_g.shape)], [0], "gather_small")[0]
    res_small = _adamw_sum(flat_w, flat_m, flat_v, parts, "adamw_small")
    like = [p[n] for n in _SMALL] + [d_full]
    un = [_unpack(r, like) for r in res_small]

    def result(kind, n):
        if n in _SHARDED:
            return jnp.stack([r[kind] for r in big[n]])
        if n == 's5_d':
            return lax.dynamic_slice(un[kind][-1], (0, me * cw), (n_b, cw))
        return un[kind][_SMALL.index(n)]

    outs = [loss, dact.reshape((1,) + dact.shape)]
    for kind in range(4):
        outs += [result(kind, n) for n in _NAMES]
    return tuple(outs)
```

```python
import functools
import math

import jax
import jax.numpy as jnp
from jax import lax
from jax.experimental import pallas as pl
from jax.experimental.pallas import tpu as pltpu

F32 = jnp.float32
BF16 = jnp.bfloat16
MESH = pl.DeviceIdType.MESH

HEAD_DIM = 128
S5_GROUP = 16
S5_STATE = 64
S5_PACK = 8
EPS = 1e-6
ADAM_LR, ADAM_B1, ADAM_B2, ADAM_EPS, ADAM_WD, ADAM_STEP = 0.001, 0.9, 0.999, 1e-08, 0.01, 10
N_DEV = 8
VMEM_LIMIT_BYTES = 56 * 1024 * 1024
SB_TILE = 256
LANE = 128


def _params(*sem):
    if sem:
        return pltpu.CompilerParams(dimension_semantics=sem, vmem_limit_bytes=VMEM_LIMIT_BYTES)
    return pltpu.CompilerParams(vmem_limit_bytes=VMEM_LIMIT_BYTES)


def _pick(dim, target, align=LANE):
    if dim <= target:
        return dim
    best = None
    for t in range(align, target + 1, align):
        if dim % t == 0:
            best = t
    assert best is not None, (dim, target, align)
    return best


MM_OPERAND_BYTES = 20 * 1024 * 1024
ANY_SPEC = pl.BlockSpec(memory_space=pl.ANY)


class _Side:
    def __init__(self, ins, out_shape, scratch, start, finish):
        self.ins, self.out_shape, self.scratch, self.start, self.finish = ins, out_shape, scratch, start, finish


def _with_side(side, n_in, n_out, n_scratch, refs, first, last, compute):
    if side is None:
        compute(refs)
        return
    ns_in, ns_out = len(side.ins), len(side.out_shape)
    own_in, s_in = refs[:n_in], refs[n_in:n_in + ns_in]
    o0 = n_in + ns_in
    own_out, s_out = refs[o0:o0 + n_out], refs[o0 + n_out:o0 + n_out + ns_out]
    c0 = o0 + n_out + ns_out
    own_scr, s_scr = refs[c0:c0 + n_scratch], refs[c0 + n_scratch:]

    @pl.when(first)
    def _():
        side.start(s_in, s_out, s_scr)

    compute(tuple(own_in) + tuple(own_out) + tuple(own_scr))

    @pl.when(last)
    def _():
        side.finish(s_in, s_out, s_scr)


def _side_call(body_own, side, *, name, grid, in_specs, out_specs, out_shape, scratch, args, sem):
    n_in, n_out, n_scr = len(args), len(out_shape), len(scratch)
    if side is not None:
        in_specs = list(in_specs) + [ANY_SPEC] * len(side.ins)
        out_specs = list(out_specs) + [ANY_SPEC] * len(side.out_shape)
        out_shape = list(out_shape) + list(side.out_shape)
        scratch = list(scratch) + list(side.scratch)
        args = list(args) + list(side.ins)
        sem = ("arbitrary",) * len(grid)

    def body(*refs):
        ids = [pl.program_id(d) for d in range(len(grid))]
        first, last = ids[0] == 0, ids[0] == grid[0] - 1
        for d in range(1, len(grid)):
            first = jnp.logical_and(first, ids[d] == 0)
            last = jnp.logical_and(last, ids[d] == grid[d] - 1)
        _with_side(side, n_in, n_out, n_scr, refs, first, last, body_own)

    res = pl.pallas_call(body, name=name, grid=grid, in_specs=in_specs, out_specs=out_specs,
                         out_shape=out_shape, scratch_shapes=scratch, compiler_params=_params(*sem))(*args)
    return res[:n_out], res[n_out:]


def _run_side(side, name):
    def body(*refs):
        ni, no = len(side.ins), len(side.out_shape)
        side.start(refs[:ni], refs[ni:ni + no], refs[ni + no:])
        side.finish(refs[:ni], refs[ni:ni + no], refs[ni + no:])

    return pl.pallas_call(body, name=name, in_specs=[ANY_SPEC] * len(side.ins),
                          out_specs=[ANY_SPEC] * len(side.out_shape), out_shape=list(side.out_shape),
                          scratch_shapes=list(side.scratch))(*side.ins)


def _mm(a, b, *, ta=False, tb=False, out_dtype=F32, res=None, scale=None, name, side=None):
    if ta:
        K, M = a.shape
    else:
        M, K = a.shape
    if tb:
        N, Kb = b.shape
    else:
        Kb, N = b.shape
    assert K == Kb, (a.shape, b.shape, ta, tb)
    tm, tn = _pick(M, 1024), _pick(N, 1024)
    per_k = 2 * (tm * a.dtype.itemsize + tn * b.dtype.itemsize)
    tk = _pick(K, max(LANE, MM_OPERAND_BYTES // per_k))
    nk = K // tk
    dims = (((0 if ta else 1,), (1 if tb else 0,)), ((), ()))

    def finish(o, r_ref, o_ref):
        if scale is not None:
            o = o * scale
        if r_ref is not None:
            o = r_ref[...].astype(F32) + o
        o_ref[...] = o.astype(o_ref.dtype)

    def compute(refs):
        a_ref, b_ref = refs[0], refs[1]
        r_ref = refs[2] if res is not None else None
        o_ref = refs[3] if res is not None else refs[2]
        part = lax.dot_general(a_ref[...].astype(BF16), b_ref[...].astype(BF16), dims, preferred_element_type=F32)
        if nk == 1:
            finish(part, r_ref, o_ref)
            return
        acc = refs[-1]
        k = pl.program_id(2)

        @pl.when(k == 0)
        def _():
            acc[...] = part

        @pl.when(k > 0)
        def _():
            acc[...] += part

        @pl.when(k == nk - 1)
        def _():
            finish(acc[...], r_ref, o_ref)

    a_spec = pl.BlockSpec((tk, tm), lambda i, j, k: (k, i)) if ta else pl.BlockSpec((tm, tk), lambda i, j, k: (i, k))
    b_spec = pl.BlockSpec((tn, tk), lambda i, j, k: (j, k)) if tb else pl.BlockSpec((tk, tn), lambda i, j, k: (k, j))
    o_spec = pl.BlockSpec((tm, tn), lambda i, j, k: (i, j))
    in_specs, args = [a_spec, b_spec], [a, b]
    if res is not None:
        in_specs.append(o_spec)
        args.append(res)
    own, extra = _side_call(
        compute, side, name=name, grid=(M // tm, N // tn, nk), in_specs=in_specs, out_specs=[o_spec],
        out_shape=[jax.ShapeDtypeStruct((M, N), out_dtype)],
        scratch=[pltpu.VMEM((tm, tn), F32)] if nk > 1 else [], args=args,
        sem=("parallel", "parallel", "arbitrary"))
    return own[0] if side is None else (own[0], extra)


def _col_chunks(width, step=2 * LANE):
    return [(lo, min(lo + step, width)) for lo in range(0, width, step)]


def _silu_grads(g):
    sg = jax.nn.sigmoid(g)
    return g * sg, sg * (1.0 + g * (1.0 - sg))


def _mm_swiglu(h, w_gu, name, side=None):
    M, K = h.shape
    n2 = w_gu.shape[1]
    c = n2 // N_DEV
    tm = _pick(M, 512)

    def compute(refs):
        h_ref, w_ref, gu_ref, s_ref = refs
        r = jnp.dot(h_ref[...], w_ref[...], preferred_element_type=F32)
        gu_ref[...] = r.astype(gu_ref.dtype)
        act, _ = _silu_grads(r[:, :c])
        s_ref[...] = (act * r[:, c:]).astype(s_ref.dtype)

    own, extra = _side_call(
        compute, side, name=name, grid=(N_DEV // 2, M // tm),
        in_specs=[pl.BlockSpec((tm, K), lambda j, i: (i, 0)), pl.BlockSpec((K, 2 * c), lambda j, i: (0, j))],
        out_specs=[pl.BlockSpec((tm, 2 * c), lambda j, i: (i, j)), pl.BlockSpec((tm, c), lambda j, i: (i, j))],
        out_shape=[jax.ShapeDtypeStruct((M, n2), BF16), jax.ShapeDtypeStruct((M, n2 // 2), BF16)],
        scratch=[], args=[h, w_gu], sem=("parallel", "parallel"))
    return own[0], own[1], extra


def _mm_dswiglu(dout, w_down, gu, name, side=None):
    M, D = dout.shape
    F_ = w_down.shape[0]
    c = F_ // (N_DEV // 2)
    tm = _pick(M, 512)
    nt = (((1,), (1,)), ((), ()))

    def compute(refs):
        d_ref, w_ref, gu_ref, o_ref = refs
        d = d_ref[...]
        for lo, hi in _col_chunks(c):
            ds = 0.5 * lax.dot_general(d, w_ref[lo:hi, :], nt, preferred_element_type=F32)
            g, u = gu_ref[:, lo:hi].astype(F32), gu_ref[:, c + lo:c + hi].astype(F32)
            act, dact = _silu_grads(g)
            o_ref[:, lo:hi] = (ds * u * dact).astype(o_ref.dtype)
            o_ref[:, c + lo:c + hi] = (ds * act).astype(o_ref.dtype)

    own, extra = _side_call(
        compute, side, name=name, grid=(N_DEV // 2, M // tm),
        in_specs=[pl.BlockSpec((tm, D), lambda j, i: (i, 0)), pl.BlockSpec((c, D), lambda j, i: (j, 0)),
                  pl.BlockSpec((tm, 2 * c), lambda j, i: (i, j))],
        out_specs=[pl.BlockSpec((tm, 2 * c), lambda j, i: (i, j))],
        out_shape=[jax.ShapeDtypeStruct((M, 2 * F_), BF16)],
        scratch=[], args=[dout, w_down, gu], sem=("parallel", "parallel"))
    return own[0], extra


class _Whole:
    def __init__(self, a):
        self.a = a


def _ew(fn, ins, outs, accs=(), *, rows, cols, tl, tc=None, name, into=None):
    tc = cols if tc is None else tc
    nj, ni = cols // tc, rows // tl
    assert nj * tc == cols and ni * tl == rows, (rows, cols, tl, tc)
    in_specs, args = [], []
    for it in ins:
        if isinstance(it, _Whole):
            in_specs.append(pl.BlockSpec(it.a.shape, lambda j, i, _n=it.a.ndim: (0,) * _n))
            args.append(it.a)
        else:
            arr, ro, co = it if isinstance(it, tuple) else (it, 0, 0)
            in_specs.append(pl.BlockSpec((tl, tc), lambda j, i, _ro=ro, _co=co: (i + _ro, j + _co)))
            args.append(arr)
    n_in, n_out = len(args), len(outs)
    out_rows, out_off, prev = (rows, 0, None) if into is None else into
    out_shape = [jax.ShapeDtypeStruct((out_rows, c), dt) for c, dt in outs]
    out_specs = [pl.BlockSpec((tl, tc), lambda j, i: (i + out_off, j)) for _ in outs]
    for c, t in accs:
        out_shape.append(jax.ShapeDtypeStruct((1, c), F32))
        out_specs.append(pl.BlockSpec((1, t), lambda j, i: (0, j)))
    aliases = {}
    if prev is not None:
        in_specs += [ANY_SPEC] * n_out
        args += list(prev)
        aliases = {n_in + k: k for k in range(n_out)}
    n_args = len(args)

    def body(*refs):
        j, i = pl.program_id(0), pl.program_id(1)
        vals = fn(j, *[r[...] for r in refs[:n_in]])
        if not isinstance(vals, (tuple, list)):
            vals = (vals,)
        for r, v in zip(refs[n_args:n_args + n_out], vals[:n_out]):
            r[...] = v.astype(r.dtype)
        for r, v in zip(refs[n_args + n_out:], vals[n_out:]):
            @pl.when(i == 0)
            def _(r=r):
                r[...] = jnp.zeros_like(r)
            r[...] += v

    res = pl.pallas_call(
        body, name=name, grid=(nj, ni), in_specs=in_specs, out_specs=out_specs, out_shape=out_shape,
        input_output_aliases=aliases, compiler_params=_params("parallel", "arbitrary"),
    )(*args)
    return res


def _row_tile(rows, bytes_per_row, budget=10 * 1024 * 1024, align=16):
    cap = max(align, budget // max(1, bytes_per_row))
    best = None
    for t in range(align, min(rows, cap) + 1, align):
        if rows % t == 0:
            best = t
    if best is None:
        best = rows
    return best


def _cast_bf16(w, layer, name):
    _, rows, cols = w.shape
    tl = _row_tile(rows, cols * 6)
    return _ew(lambda j, b: b, [(w.reshape(-1, cols), layer * (rows // tl), 0)], [(cols, BF16)],
               rows=rows, cols=cols, tl=tl, name=name)[0]


def _rmsnorm_fwd(x, g, name, out_dtype=BF16):
    rows, cols = x.shape

    def fn(j, xb, gb):
        r = lax.rsqrt(jnp.mean(xb * xb, axis=-1, keepdims=True) + EPS)
        return (xb * r * gb,)

    tl = _row_tile(rows, cols * 12)
    return _ew(fn, [x, _Whole(g.reshape(1, cols))], [(cols, out_dtype)], rows=rows, cols=cols, tl=tl, name=name)[0]


def _rmsnorm_bwd(dh, x, g, dres, name):
    rows, cols = x.shape

    def fn(j, dhb, xb, gb, *rest):
        r = lax.rsqrt(jnp.mean(xb * xb, axis=-1, keepdims=True) + EPS)
        xh = xb * r
        dxh = dhb * gb
        dx = r * (dxh - xh * jnp.mean(dxh * xh, axis=-1, keepdims=True))
        if rest:
            dx = dx + rest[0]
        return dx, dx, jnp.sum(dhb * xh, axis=0, keepdims=True)

    ins = [dh, x, _Whole(g.reshape(1, cols))] + ([dres] if dres is not None else [])
    tl = _row_tile(rows, cols * 28)
    return _ew(fn, ins, [(cols, F32), (cols, BF16)], [(cols, cols)], rows=rows, cols=cols, tl=tl, name=name)


_GELU_K = math.sqrt(2.0 / math.pi)
_GELU_C = 0.044715


def _gelu(y):
    return 0.5 * y * (1.0 + jnp.tanh(_GELU_K * (y + _GELU_C * y * y * y)))


def _gelu_grad(y):
    t = jnp.tanh(_GELU_K * (y + _GELU_C * y * y * y))
    return 0.5 * (1.0 + t) + 0.5 * y * (1.0 - t * t) * _GELU_K * (1.0 + 3.0 * _GELU_C * y * y)


def _loss_fwd_bwd(y, target, name):
    rows, cols = y.shape

    def fn(j, yb, tb):
        e = yb - tb
        part = 0.5 * jnp.sum(e * e) / cols
        dy = e * (1.0 / cols)
        return dy, dy, jnp.full((1, LANE), part, F32)

    tl = _row_tile(rows, cols * 20)
    dy, dy16, loss = _ew(fn, [y, target], [(cols, F32), (cols, BF16)], [(LANE, LANE)], rows=rows, cols=cols,
                         tl=tl, name=name)
    return loss, dy, dy16


SB_SCALE = 1.0 / math.sqrt(HEAD_DIM)
SB_ROWS = 4


def _sb_terms(z):
    z2 = z * (SB_SCALE / math.log(2.0))
    t = jnp.log2(1.0 + jnp.exp2(-jnp.abs(z2)))
    return jnp.maximum(z2, 0.0) + t, jnp.minimum(z2, 0.0) - t


def _strict_lower(n, m):
    return lax.broadcasted_iota(jnp.int32, (n, m), 0) > lax.broadcasted_iota(jnp.int32, (n, m), 1)


def _sb_tiles(L):
    T = min(SB_TILE, L)
    R = max(r for r in (SB_ROWS, 2, 1) if L % (r * T) == 0)
    return T, R


def _sb_fwd(proj, n_heads, name):
    L = proj.shape[0]
    T, R = _sb_tiles(L)
    M = R * T
    H = n_heads
    nt = (((1,), (1,)), ((), ()))

    def body(q_ref, k_ref, v_ref, o_ref, s_ref, acc_ref, later_ref):
        i = pl.program_id(1)
        acc_ref[...] = jnp.zeros_like(acc_ref)
        later_ref[...] = jnp.zeros_like(later_ref)
        upper = _strict_lower(T, T).astype(BF16)

        def strip(row0, nrows, jb, diagonal):
            rows = slice(row0, row0 + nrows)
            start = pl.multiple_of(jb * T, T)
            kb, vb = k_ref[pl.ds(start, T), :], v_ref[pl.ds(start, T), :]
            z = lax.dot_general(q_ref[rows, :], kb, nt, preferred_element_type=F32)
            sp, logsig = _sb_terms(z)
            mask = _strict_lower(nrows, T) if diagonal else None
            spb = (jnp.where(mask, sp, 0.0) if diagonal else sp).astype(BF16)
            between = jnp.dot(spb, upper, preferred_element_type=F32)
            later = later_ref[rows, :]
            w = jnp.exp2(logsig - between - later)
            if diagonal:
                w = jnp.where(mask, w, 0.0)
            acc_ref[rows, :] += jnp.dot(w.astype(BF16), vb, preferred_element_type=F32)
            later_ref[rows, :] = later + between[:, 0:1] + spb[:, 0:1].astype(F32)

        for d in reversed(range(R)):
            strip(d * T, M - d * T, R * i + d, True)

        def step(it, carry):
            strip(0, M, R * i - 1 - it, False)
            return carry

        lax.fori_loop(0, R * i, step, 0)
        o_ref[...] = acc_ref[...].astype(o_ref.dtype)
        s_ref[...] = jnp.broadcast_to(later_ref[...], (M, HEAD_DIM))

    return pl.pallas_call(
        body, name=name, grid=(H, L // M),
        in_specs=[pl.BlockSpec((M, HEAD_DIM), lambda h, i: (i, h)),
                  pl.BlockSpec((L, HEAD_DIM), lambda h, i: (0, H + h)),
                  pl.BlockSpec((L, HEAD_DIM), lambda h, i: (0, 2 * H + h))],
        out_specs=[pl.BlockSpec((M, HEAD_DIM), lambda h, i: (i, h)),
                   pl.BlockSpec((M, HEAD_DIM), lambda h, i: (i, h))],
        out_shape=[jax.ShapeDtypeStruct((L, H * HEAD_DIM), BF16), jax.ShapeDtypeStruct((L, H * HEAD_DIM), F32)],
        scratch_shapes=[pltpu.VMEM((M, HEAD_DIM), F32), pltpu.VMEM((M, 1), F32)],
        compiler_params=_params("parallel", "arbitrary"),
    )(proj, proj, proj)


def _sb_bwd(proj, do, rowsum, n_heads, name):
    L = proj.shape[0]
    T, R = _sb_tiles(L)
    M = R * T
    H = n_heads
    nt = (((1,), (1,)), ((), ()))
    tn = (((0,), (0,)), ((), ()))

    def body(q_ref, k_ref, v_ref, do_ref, s_ref, dq_out, dk_out, dv_out,
             do16_ref, upto_ref, hsum_ref, dq_ref, dk_ref, dv_ref):
        i = pl.program_id(1)

        @pl.when(i == 0)
        def _():
            dk_ref[...] = jnp.zeros_like(dk_ref)
            dv_ref[...] = jnp.zeros_like(dv_ref)

        dq_ref[...] = jnp.zeros_like(dq_ref)
        upto_ref[...] = jnp.zeros_like(upto_ref)
        hsum_ref[...] = jnp.zeros_like(hsum_ref)
        do16_ref[...] = do_ref[...].astype(BF16)
        upper = _strict_lower(T, T).astype(BF16)
        lower_inc = jnp.logical_not(_strict_lower(T, T)).astype(BF16)

        def strip(row0, nrows, jb, diagonal):
            rows = slice(row0, row0 + nrows)
            start = pl.multiple_of(jb * T, T)
            kb, vb = k_ref[pl.ds(start, T), :], v_ref[pl.ds(start, T), :]
            q, dob = q_ref[rows, :], do16_ref[rows, :]
            z = lax.dot_general(q, kb, nt, preferred_element_type=F32)
            sp, logsig = _sb_terms(z)
            mask = _strict_lower(nrows, T) if diagonal else None
            spb = (jnp.where(mask, sp, 0.0) if diagonal else sp).astype(BF16)
            between = jnp.dot(spb, upper, preferred_element_type=F32)
            upto = upto_ref[rows, :] + between[:, 0:1] + spb[:, 0:1].astype(F32)
            upto_ref[rows, :] = upto
            a = jnp.exp2(logsig - between - (s_ref[rows, 0:1] - upto))
            if diagonal:
                a = jnp.where(mask, a, 0.0)
            g = a * lax.dot_general(dob, vb, nt, preferred_element_type=F32)
            hcum = jnp.dot(g.astype(BF16), lower_inc, preferred_element_type=F32) + hsum_ref[rows, :]
            hsum_ref[rows, :] = hcum[:, T - 1:T]
            dz = (g - jnp.exp2(logsig) * hcum) * SB_SCALE
            if diagonal:
                dz = jnp.where(mask, dz, 0.0)
            dzb, ab = dz.astype(BF16), a.astype(BF16)
            dq_ref[rows, :] += jnp.dot(dzb, kb, preferred_element_type=F32)
            dk_ref[pl.ds(start, T), :] += lax.dot_general(dzb, q, tn, preferred_element_type=F32)
            dv_ref[pl.ds(start, T), :] += lax.dot_general(ab, dob, tn, preferred_element_type=F32)

        def step(jb, carry):
            strip(0, M, jb, False)
            return carry

        lax.fori_loop(0, R * i, step, 0)
        for d in range(R):
            strip(d * T, M - d * T, R * i + d, True)
        dq_out[...] = dq_ref[...].astype(dq_out.dtype)

        @pl.when(i == L // M - 1)
        def _():
            dk_out[...] = dk_ref[...].astype(dk_out.dtype)
            dv_out[...] = dv_ref[...].astype(dv_out.dtype)

    shp = jax.ShapeDtypeStruct((L, H * HEAD_DIM), BF16)
    return pl.pallas_call(
        body, name=name, grid=(H, L // M),
        in_specs=[pl.BlockSpec((M, HEAD_DIM), lambda h, i: (i, h)),
                  pl.BlockSpec((L, HEAD_DIM), lambda h, i: (0, H + h)),
                  pl.BlockSpec((L, HEAD_DIM), lambda h, i: (0, 2 * H + h)),
                  pl.BlockSpec((M, HEAD_DIM), lambda h, i: (i, h)),
                  pl.BlockSpec((M, HEAD_DIM), lambda h, i: (i, h))],
        out_specs=[pl.BlockSpec((M, HEAD_DIM), lambda h, i: (i, h)),
                   pl.BlockSpec((L, HEAD_DIM), lambda h, i: (0, h)),
                   pl.BlockSpec((L, HEAD_DIM), lambda h, i: (0, h))],
        out_shape=[shp, shp, shp],
        scratch_shapes=[pltpu.VMEM((M, HEAD_DIM), BF16), pltpu.VMEM((M, 1), F32), pltpu.VMEM((M, 1), F32),
                        pltpu.VMEM((M, HEAD_DIM), F32), pltpu.VMEM((L, HEAD_DIM), F32),
                        pltpu.VMEM((L, HEAD_DIM), F32)],
        compiler_params=_params("parallel", "arbitrary"),
    )(proj, proj, proj, do, rowsum)


def _head_norm(v):
    r = lax.rsqrt(jnp.mean(v * v, axis=-1, keepdims=True) + EPS)
    return v * r, r


def _memkv_fwd(kv, gk, name):
    M, w2 = kv.shape
    mw = w2 // 2

    def body(kv_ref, gk_ref, k_ref, v_ref):
        for h in range(mw // HEAD_DIM):
            sl = slice(h * HEAD_DIM, (h + 1) * HEAD_DIM)
            xh, _ = _head_norm(kv_ref[:, sl])
            k_ref[:, sl] = xh * gk_ref[...]
        v_ref[...] = kv_ref[:, mw:]

    return pl.pallas_call(body, name=name, out_shape=[jax.ShapeDtypeStruct((M, mw), F32)] * 2,
                          compiler_params=_params())(kv, gk.reshape(1, HEAD_DIM))


def _memkv_bwd(kv, gk, dk, dv, name):
    M, w2 = kv.shape
    mw = w2 // 2

    def body(kv_ref, gk_ref, dk_ref, dv_ref, dkv_ref, dg_ref):
        dg = jnp.zeros((1, HEAD_DIM), F32)
        for h in range(mw // HEAD_DIM):
            sl = slice(h * HEAD_DIM, (h + 1) * HEAD_DIM)
            xh, r = _head_norm(kv_ref[:, sl])
            d = dk_ref[:, sl]
            dg = dg + jnp.sum(d * xh, axis=0, keepdims=True)
            dxh = d * gk_ref[...]
            dkv_ref[:, sl] = r * (dxh - xh * jnp.mean(dxh * xh, axis=-1, keepdims=True))
        dkv_ref[:, mw:] = dv_ref[...]
        dg_ref[...] = dg

    return pl.pallas_call(body, name=name,
                          out_shape=[jax.ShapeDtypeStruct((M, w2), F32), jax.ShapeDtypeStruct((1, HEAD_DIM), F32)],
                          compiler_params=_params())(kv, gk.reshape(1, HEAD_DIM), dk, dv)


def _mem_tile(L):
    return _pick(L, 512, 8)


def _memattn_fwd(proj, khat, v, gq, name):
    L, W = proj.shape
    M, mw = khat.shape
    assert (W - mw) % mw == 0
    qcol = (W - mw) // mw
    tl = _mem_tile(L)
    scale = 1.0 / math.sqrt(HEAD_DIM)

    def body(q_ref, k_ref, v_ref, g_ref, o_ref):
        for h in range(mw // HEAD_DIM):
            sl = slice(h * HEAD_DIM, (h + 1) * HEAD_DIM)
            xh, _ = _head_norm(q_ref[:, sl].astype(F32))
            qh = (xh * g_ref[...]).astype(BF16)
            s = lax.dot_general(qh, k_ref[:, sl].astype(BF16), (((1,), (1,)), ((), ())),
                                preferred_element_type=F32) * scale
            e = jnp.exp(s - jnp.max(s, axis=-1, keepdims=True))
            p = e / jnp.sum(e, axis=-1, keepdims=True)
            o_ref[:, sl] = jnp.dot(p.astype(BF16), v_ref[:, sl].astype(BF16),
                                   preferred_element_type=F32).astype(o_ref.dtype)

    whole = lambda a: pl.BlockSpec(a.shape, lambda i: (0,) * a.ndim)
    g2 = gq.reshape(1, HEAD_DIM)
    return pl.pallas_call(
        body, name=name, grid=(L // tl,),
        in_specs=[pl.BlockSpec((tl, mw), lambda i: (i, qcol)), whole(khat), whole(v), whole(g2)],
        out_specs=pl.BlockSpec((tl, mw), lambda i: (i, 0)),
        out_shape=jax.ShapeDtypeStruct((L, mw), BF16),
        compiler_params=_params("parallel"),
    )(proj, khat, v, g2)


def _memattn_bwd(proj, khat, v, gq, dcat, name):
    L, W = proj.shape
    M, mw = khat.shape
    qcol = (W - mw) // mw
    dcol = (dcat.shape[1] - mw) // mw
    tl = _mem_tile(L)
    scale = 1.0 / math.sqrt(HEAD_DIM)

    def body(q_ref, k_ref, v_ref, g_ref, do_ref, dq_ref, dk_ref, dv_ref, dg_ref):
        i = pl.program_id(0)

        @pl.when(i == 0)
        def _():
            dk_ref[...] = jnp.zeros_like(dk_ref)
            dv_ref[...] = jnp.zeros_like(dv_ref)
            dg_ref[...] = jnp.zeros_like(dg_ref)

        for h in range(mw // HEAD_DIM):
            sl = slice(h * HEAD_DIM, (h + 1) * HEAD_DIM)
            xh, r = _head_norm(q_ref[:, sl].astype(F32))
            qh = (xh * g_ref[...]).astype(BF16)
            kb = k_ref[:, sl].astype(BF16)
            vb = v_ref[:, sl].astype(BF16)
            dob = do_ref[:, sl].astype(BF16)
            s = lax.dot_general(qh, kb, (((1,), (1,)), ((), ())), preferred_element_type=F32) * scale
            e = jnp.exp(s - jnp.max(s, axis=-1, keepdims=True))
            p = e / jnp.sum(e, axis=-1, keepdims=True)
            dv_ref[:, sl] += lax.dot_general(p.astype(BF16), dob, (((0,), (0,)), ((), ())),
                                             preferred_element_type=F32)
            dp = lax.dot_general(dob, vb, (((1,), (1,)), ((), ())), preferred_element_type=F32)
            ds = (p * (dp - jnp.sum(dp * p, axis=-1, keepdims=True)) * scale).astype(BF16)
            dqh = jnp.dot(ds, kb, preferred_element_type=F32)
            dk_ref[:, sl] += lax.dot_general(ds, qh, (((0,), (0,)), ((), ())), preferred_element_type=F32)
            dg_ref[...] += jnp.sum(dqh * xh, axis=0, keepdims=True)
            dxh = dqh * g_ref[...]
            dq_ref[:, sl] = (r * (dxh - xh * jnp.mean(dxh * xh, axis=-1, keepdims=True))).astype(dq_ref.dtype)

    whole = lambda a: pl.BlockSpec(a.shape, lambda i: (0,) * a.ndim)
    g2 = gq.reshape(1, HEAD_DIM)
    kshape = jax.ShapeDtypeStruct((M, mw), F32)
    return pl.pallas_call(
        body, name=name, grid=(L // tl,),
        in_specs=[pl.BlockSpec((tl, mw), lambda i: (i, qcol)), whole(khat), whole(v), whole(g2),
                  pl.BlockSpec((tl, mw), lambda i: (i, dcol))],
        out_specs=[pl.BlockSpec((tl, mw), lambda i: (i, 0)), whole(kshape), whole(kshape),
                   pl.BlockSpec((1, HEAD_DIM), lambda i: (0, 0))],
        out_shape=[jax.ShapeDtypeStruct((L, mw), BF16), kshape, kshape, jax.ShapeDtypeStruct((1, HEAD_DIM), F32)],
        compiler_params=_params("arbitrary"),
    )(proj, khat, v, g2, dcat)


def _cmul(ar, ai, br, bi):
    return ar * br - ai * bi, ar * bi + ai * br


def _s5_discretize(log_dt, a_re, a_im):
    dt = jnp.exp(log_dt)
    mag = jnp.exp(a_re * dt)
    ab_re, ab_im = mag * jnp.cos(a_im * dt), mag * jnp.sin(a_im * dt)
    den = a_re * a_re + a_im * a_im
    inv_re, inv_im = a_re / den, -a_im / den
    f_re, f_im = _cmul(ab_re - 1.0, ab_im, inv_re, inv_im)
    return dt, ab_re, ab_im, inv_re, inv_im, f_re, f_im


def _s5_prep_fwd(log_dt, a_re, a_im, bt_re, bt_im, name):
    G, N = a_re.shape
    C = bt_re.shape[0]

    def body(ld_ref, ar_ref, ai_ref, br_ref, bi_ref, pr_ref, pi_ref, bbr_ref, bbi_ref):
        _, ab_re, ab_im, _, _, f_re, f_im = _s5_discretize(ld_ref[...], ar_ref[...], ai_ref[...])
        p_re, p_im = ab_re, ab_im
        for e in range(8):
            pr_ref[e] = p_re
            pi_ref[e] = p_im
            p_re, p_im = _cmul(p_re, p_im, ab_re, ab_im)
        for ch in range(C):
            bbr_ref[ch], bbi_ref[ch] = _cmul(f_re, f_im, br_ref[ch], bi_ref[ch])

    pw = jax.ShapeDtypeStruct((8, G, N), F32)
    bb = jax.ShapeDtypeStruct((C, G, N), F32)
    return pl.pallas_call(body, name=name, out_shape=[pw, pw, bb, bb], compiler_params=_params())(
        log_dt.reshape(G, 1), a_re, a_im, bt_re, bt_im)


def _s5_prep_bwd(log_dt, a_re, a_im, bt_re, bt_im, ga_re, ga_im, gbb_re, gbb_im, name):
    G, N = a_re.shape
    C = bt_re.shape[0]

    def body(ld_ref, ar_ref, ai_ref, br_ref, bi_ref, gar_ref, gai_ref, gbr_ref, gbi_ref,
             gld_ref, gare_ref, gaim_ref, gbtr_ref, gbti_ref):
        a_re_, a_im_ = ar_ref[...], ai_ref[...]
        dt, ab_re, ab_im, inv_re, inv_im, f_re, f_im = _s5_discretize(ld_ref[...], a_re_, a_im_)
        gf_re, gf_im = jnp.zeros((G, N), F32), jnp.zeros((G, N), F32)
        for ch in range(C):
            gbr, gbi = gbr_ref[ch], gbi_ref[ch]
            gbtr_ref[ch], gbti_ref[ch] = _cmul(f_re, -f_im, gbr, gbi)
            tr, ti = _cmul(br_ref[ch], -bi_ref[ch], gbr, gbi)
            gf_re, gf_im = gf_re + tr, gf_im + ti
        t_re, t_im = _cmul(gf_re, gf_im, inv_re, -inv_im)
        gab_re, gab_im = gar_ref[...] + t_re, gai_ref[...] + t_im
        u_re, u_im = _cmul(dt * ab_re, -dt * ab_im, gab_re, gab_im)
        fl_re, fl_im = _cmul(f_re, f_im, inv_re, inv_im)
        w_re, w_im = _cmul(fl_re, -fl_im, gf_re, gf_im)
        gare_ref[...] = u_re - w_re
        gaim_ref[...] = u_im - w_im
        la_re, la_im = _cmul(a_re_, a_im_, ab_re, ab_im)
        gdt = jnp.sum(la_re * gab_re + la_im * gab_im, axis=1, keepdims=True)
        gld_ref[...] = dt * gdt

    gn = jax.ShapeDtypeStruct((G, N), F32)
    bb = jax.ShapeDtypeStruct((C, G, N), F32)
    return pl.pallas_call(body, name=name,
                          out_shape=[jax.ShapeDtypeStruct((G, 1), F32), gn, gn, bb, bb],
                          compiler_params=_params())(
        log_dt.reshape(G, 1), a_re, a_im, bt_re, bt_im, ga_re, ga_im, gbb_re, gbb_im)


def _dot3(a, b, dims=(((1,), (0,)), ((), ()))):
    return lax.dot_general(a.astype(BF16), b.astype(BF16), dims, preferred_element_type=F32)


def _block_diag(t):
    nch, P, r, c = t.shape
    eye = jnp.eye(P, dtype=t.dtype)
    return (t[:, :, :, None, :] * eye[None, :, None, :, None]).reshape(nch, P * r, P * c)


def _block_diag_extract(m, P):
    nch, R, Cc = m.shape
    r, c = R // P, Cc // P
    m5 = m.reshape(nch, P, r, P, c)
    return jnp.stack([m5[:, k, :, k, :] for k in range(P)], axis=1)


def _s5_tables(pw_re, pw_im):
    pr = pw_re.reshape(8, -1)
    pi = pw_im.reshape(8, -1)
    row = jnp.arange(8)[:, None]
    tiles = []
    for d in (1, 2, 4):
        keep = row >= d
        tiles += [jnp.where(keep, pr[d - 1][None], 0.0), jnp.where(keep, pi[d - 1][None], 0.0)]
    tiles += [pr, pi]
    for d in (1, 2, 4):
        keep = row <= 7 - d
        tiles += [jnp.where(keep, pr[d - 1][None], 0.0), jnp.where(keep, -pi[d - 1][None], 0.0)]
    tiles += [pr[::-1], -pi[::-1]]
    return jnp.stack(tiles, axis=0)


def _scan_tile(sr, si, m, shifts):
    for n, sh in enumerate(shifts):
        rr, ri = pltpu.roll(sr, sh, 0), pltpu.roll(si, sh, 0)
        mr, mi = m[2 * n], m[2 * n + 1]
        sr, si = sr + mr * rr - mi * ri, si + mr * ri + mi * rr
    return sr, si


def _s5_rows(L):
    return _pick(L, 512, 8)


def _s5_fwd(proj, bd_re, bd_im, cd_re, cd_imn, tables, d_skip, tok_w, name):
    L = proj.shape[0]
    nch = tok_w // LANE
    ns = S5_PACK * S5_STATE
    tl = _s5_rows(L)
    nr = L // tl

    def body(u_ref, bdr_ref, bdi_ref, cdr_ref, cdi_ref, tab_ref, d_ref, y_ref, xr_ref, xi_ref, carry):
        r = pl.program_id(1)

        @pl.when(r == 0)
        def _():
            carry[...] = jnp.zeros_like(carry)

        u = u_ref[...]
        xr_ref[...] = _dot3(u, bdr_ref[0])
        xi_ref[...] = _dot3(u, bdi_ref[0])
        m = [tab_ref[n] for n in range(8)]

        def step(t, c):
            cr, ci = c
            rows = pl.ds(pl.multiple_of(t * 8, 8), 8)
            sr, si = _scan_tile(xr_ref[rows, :], xi_ref[rows, :], m, (1, 2, 4))
            sr, si = sr + m[6] * cr - m[7] * ci, si + m[6] * ci + m[7] * cr
            xr_ref[rows, :] = sr
            xi_ref[rows, :] = si
            return jnp.broadcast_to(sr[7:8, :], (8, ns)), jnp.broadcast_to(si[7:8, :], (8, ns))

        cr, ci = lax.fori_loop(0, tl // 8, step, (carry[0], carry[1]))
        carry[0] = cr
        carry[1] = ci
        y_ref[...] = _dot3(xr_ref[...], cdr_ref[0]) + _dot3(xi_ref[...], cdi_ref[0]) + d_ref[...] * u

    chunk3 = lambda a: pl.BlockSpec((1,) + a.shape[1:], lambda j, r: (j, 0, 0))
    return pl.pallas_call(
        body, name=name, grid=(nch, nr),
        in_specs=[pl.BlockSpec((tl, LANE), lambda j, r: (r, j)),
                  chunk3(bd_re), chunk3(bd_im), chunk3(cd_re), chunk3(cd_imn),
                  pl.BlockSpec((8, 8, ns), lambda j, r: (0, 0, j)),
                  pl.BlockSpec((1, LANE), lambda j, r: (0, j))],
        out_specs=[pl.BlockSpec((tl, LANE), lambda j, r: (r, j)),
                   pl.BlockSpec((tl, ns), lambda j, r: (r, j)),
                   pl.BlockSpec((tl, ns), lambda j, r: (r, j))],
        out_shape=[jax.ShapeDtypeStruct((L, tok_w), F32),
                   jax.ShapeDtypeStruct((L, nch * ns), F32), jax.ShapeDtypeStruct((L, nch * ns), F32)],
        scratch_shapes=[pltpu.VMEM((2, 8, ns), F32)],
        compiler_params=_params("parallel", "arbitrary"),
    )(proj, bd_re, bd_im, cd_re, cd_imn, tables, d_skip.reshape(1, tok_w))


def _s5_bwd(proj, dy, xr, xi, bd_re, bd_im, cd_re, cd_imn, tables, d_skip, tok_w, name):
    L = proj.shape[0]
    nch = tok_w // LANE
    ns = S5_PACK * S5_STATE
    tl = _s5_rows(L)
    nr = L // tl
    tn = (((0,), (0,)), ((), ()))
    nt = (((1,), (1,)), ((), ()))

    def body(u_ref, dy_ref, xr_ref, xi_ref, bdr_ref, bdi_ref, cdr_ref, cdi_ref, tab_ref, d_ref,
             du_ref, gcr_ref, gci_ref, gbr_ref, gbi_ref, gar_ref, gai_ref, gd_ref, lr, li, carry):
        r = pl.program_id(1)

        @pl.when(r == 0)
        def _():
            carry[...] = jnp.zeros_like(carry)
            for ref in (gcr_ref, gci_ref, gbr_ref, gbi_ref, gar_ref, gai_ref, gd_ref):
                ref[...] = jnp.zeros_like(ref)

        u, dyb = u_ref[...], dy_ref[...]
        lr[...] = _dot3(dyb, cdr_ref[0], nt)
        li[...] = _dot3(dyb, cdi_ref[0], nt)
        m = [tab_ref[n] for n in range(8)]
        last = lax.broadcasted_iota(jnp.int32, (8, ns), 0) == 7

        def step(it, c):
            cr, ci, ar, ai = c
            t = tl // 8 - 1 - it
            rows = pl.ds(pl.multiple_of(t * 8, 8), 8)
            sr, si = _scan_tile(lr[rows, :], li[rows, :], m, (7, 6, 4))
            sr, si = sr + m[6] * cr - m[7] * ci, si + m[6] * ci + m[7] * cr
            lr[rows, :] = sr
            li[rows, :] = si
            nr_, ni_ = jnp.where(last, cr, pltpu.roll(sr, 7, 0)), jnp.where(last, ci, pltpu.roll(si, 7, 0))
            xr_t, xi_t = xr_ref[rows, :], xi_ref[rows, :]
            ar = ar + xr_t * nr_ + xi_t * ni_
            ai = ai + xr_t * ni_ - xi_t * nr_
            return jnp.broadcast_to(sr[0:1, :], (8, ns)), jnp.broadcast_to(si[0:1, :], (8, ns)), ar, ai

        z = jnp.zeros((8, ns), F32)
        cr, ci, ar, ai = lax.fori_loop(0, tl // 8, step, (carry[0], carry[1], z, z))
        carry[0] = cr
        carry[1] = ci
        gar_ref[...] += jnp.sum(ar, axis=0, keepdims=True)
        gai_ref[...] += jnp.sum(ai, axis=0, keepdims=True)
        lam_r, lam_i = lr[...], li[...]
        du_ref[...] = (_dot3(lam_r, bdr_ref[0], nt) + _dot3(lam_i, bdi_ref[0], nt)
                       + d_ref[...] * dyb).astype(du_ref.dtype)
        gd_ref[...] += jnp.sum(dyb * u, axis=0, keepdims=True)
        gcr_ref[0] += _dot3(xr_ref[...], dyb, tn)
        gci_ref[0] += _dot3(xi_ref[...], dyb, tn)
        gbr_ref[0] += _dot3(u, lam_r, tn)
        gbi_ref[0] += _dot3(u, lam_i, tn)

    rev = lambda j, r: (nr - 1 - r, j)
    chunk3 = lambda a: pl.BlockSpec((1,) + a.shape[1:], lambda j, r: (j, 0, 0))
    gcd = jax.ShapeDtypeStruct((nch, ns, LANE), F32)
    gbd = jax.ShapeDtypeStruct((nch, LANE, ns), F32)
    gab = jax.ShapeDtypeStruct((1, nch * ns), F32)
    return pl.pallas_call(
        body, name=name, grid=(nch, nr),
        in_specs=[pl.BlockSpec((tl, LANE), rev), pl.BlockSpec((tl, LANE), rev),
                  pl.BlockSpec((tl, ns), rev), pl.BlockSpec((tl, ns), rev),
                  chunk3(bd_re), chunk3(bd_im), chunk3(cd_re), chunk3(cd_imn),
                  pl.BlockSpec((8, 8, ns), lambda j, r: (1, 0, j)),
                  pl.BlockSpec((1, LANE), lambda j, r: (0, j))],
        out_specs=[pl.BlockSpec((tl, LANE), rev), chunk3(gcd), chunk3(gcd), chunk3(gbd), chunk3(gbd),
                   pl.BlockSpec((1, ns), lambda j, r: (0, j)), pl.BlockSpec((1, ns), lambda j, r: (0, j)),
                   pl.BlockSpec((1, LANE), lambda j, r: (0, j))],
        out_shape=[jax.ShapeDtypeStruct((L, tok_w), BF16), gcd, gcd, gbd, gbd, gab, gab,
                   jax.ShapeDtypeStruct((1, tok_w), F32)],
        scratch_shapes=[pltpu.VMEM((tl, ns), F32), pltpu.VMEM((tl, ns), F32), pltpu.VMEM((2, 8, ns), F32)],
        compiler_params=_params("parallel", "arbitrary"),
    )(proj, dy, xr, xi, bd_re, bd_im, cd_re, cd_imn, tables, d_skip.reshape(1, tok_w))


def _lin(dev):
    return 4 * dev[0] + 2 * dev[1] + dev[2]


def _shard_view(ref, axis, index, size):
    ix = tuple(pl.ds(index * size, size) if d == axis else pl.ds(0, ref.shape[d]) for d in range(len(ref.shape)))
    return ref.at[ix]


def _slot_index(dev, paired):
    idx = _lin(dev)
    half = N_DEV // 2
    return 2 * (idx % half) + idx // half if paired else idx


def _comm_sems(n):
    return [pltpu.SemaphoreType.DMA((n, 7)), pltpu.SemaphoreType.DMA((n, 7)), pltpu.SemaphoreType.DMA((n,))]


def _gather_side(shards, axes, paired=None):
    n = len(shards)
    paired = [False] * n if paired is None else paired
    sizes = [s.shape[a] for s, a in zip(shards, axes)]

    def tools(ins, outs, sems):
        send_sems, recv_sems, local_sems = sems
        x, y, c = lax.axis_index("x"), lax.axis_index("y"), lax.axis_index("c")
        me, sibling = (x, y, c), (x, y, 1 - c)
        chips = [(1 - x, y), (x, 1 - y), (1 - x, 1 - y)]

        def slot(a, dev):
            return _shard_view(outs[a], axes[a], _slot_index(dev, paired[a]), sizes[a])

        def copy(a, k, block, to, src=None):
            return pltpu.make_async_remote_copy(
                src_ref=slot(a, block) if src is None else src, dst_ref=slot(a, block),
                send_sem=send_sems.at[a, k], recv_sem=recv_sems.at[a, k],
                device_id=to, device_id_type=MESH)

        mine = [pltpu.make_async_copy(ins[a], slot(a, me), local_sems.at[a]) for a in range(n)]
        first = []
        for a in range(n):
            first.append(copy(a, 0, me, sibling, src=ins[a]))
            first += [copy(a, 1 + j, me, (*chip, c), src=ins[a]) for j, chip in enumerate(chips)]
        return me, sibling, chips, c, copy, mine, first

    def start(ins, outs, sems):
        _, _, _, _, _, mine, first = tools(ins, outs, sems)
        for cp in mine + first:
            cp.start()

    def finish(ins, outs, sems):
        me, sibling, chips, c, copy, mine, first = tools(ins, outs, sems)
        passed = []
        for j, chip in enumerate(chips):
            for a in range(n):
                copy(a, 1 + j, (*chip, c), me).wait_recv()
                cp = copy(a, 4 + j, (*chip, c), sibling)
                cp.start()
                passed.append(cp)
        for a in range(n):
            copy(a, 0, sibling, me).wait_recv()
            for j, chip in enumerate(chips):
                copy(a, 4 + j, (*chip, 1 - c), me).wait_recv()
        for cp in first + passed:
            cp.wait_send()
        for cp in mine:
            cp.wait()

    out_shape = []
    for s, a in zip(shards, axes):
        shp = list(s.shape)
        shp[a] *= N_DEV
        out_shape.append(jax.ShapeDtypeStruct(tuple(shp), s.dtype))
    return _Side(list(shards), out_shape, _comm_sems(n), start, finish)


def _scatter_side(fulls, axes, paired=None):
    n = len(fulls)
    paired = [False] * n if paired is None else paired
    sizes = [f.shape[a] // N_DEV for f, a in zip(fulls, axes)]
    masks = [(mx, my, mc) for mx in (0, 1) for my in (0, 1) for mc in (0, 1)][1:]

    def copies(ins, outs, sems):
        send_sems, recv_sems, local_sems = sems
        x, y, c = lax.axis_index("x"), lax.axis_index("y"), lax.axis_index("c")
        me = (x, y, c)
        flip = lambda v, m: 1 - v if m else v
        piece = lambda a, dev: _shard_view(ins[a], axes[a], _slot_index(dev, paired[a]), sizes[a])
        local = [pltpu.make_async_copy(piece(a, me), outs[a].at[_lin(me)], local_sems.at[a]) for a in range(n)]
        sends = []
        for k, (mx, my, mc) in enumerate(masks):
            peer = (flip(x, mx), flip(y, my), flip(c, mc))
            for a in range(n):
                sends.append(pltpu.make_async_remote_copy(
                    src_ref=piece(a, peer), dst_ref=outs[a].at[_lin(me)],
                    send_sem=send_sems.at[a, k], recv_sem=recv_sems.at[a, k],
                    device_id=peer, device_id_type=MESH))
        return local, sends

    def start(ins, outs, sems):
        local, sends = copies(ins, outs, sems)
        for cp in local + sends:
            cp.start()

    def finish(ins, outs, sems):
        local, sends = copies(ins, outs, sems)
        for cp in sends:
            cp.wait()
        for cp in local:
            cp.wait()

    out_shape = []
    for f, a, sz in zip(fulls, axes, sizes):
        shp = list(f.shape)
        shp[a] = sz
        out_shape.append(jax.ShapeDtypeStruct((N_DEV,) + tuple(shp), f.dtype))
    return _Side(list(fulls), out_shape, _comm_sems(n), start, finish)


def _all_gather(shards, axes, name, paired=None):
    return _run_side(_gather_side(shards, axes, paired), name)


def _adamw_sum(w, m, v, parts, name, layer=0, prev=None):
    rows, cols = parts.shape[1:]
    n_layers = w.shape[0] // rows
    c1 = 1.0 / (1.0 - ADAM_B1 ** ADAM_STEP)
    c2 = 1.0 / (1.0 - ADAM_B2 ** ADAM_STEP)

    def fn(j, wb, mb, vb, *ps):
        g = ps[0].astype(F32)
        for p in ps[1:]:
            g = g + p.astype(F32)
        mn = ADAM_B1 * mb + (1.0 - ADAM_B1) * g
        vn = ADAM_B2 * vb + (1.0 - ADAM_B2) * (g * g)
        delta = -ADAM_LR * ((mn * c1) / (jnp.sqrt(vn * c2) + ADAM_EPS) + ADAM_WD * wb)
        return g, delta, mn, vn

    per_row = cols * (12 + 16 + N_DEV * parts.dtype.itemsize)
    tl = _row_tile(rows, per_row, budget=12 * 1024 * 1024)
    flat = parts.reshape(N_DEV * rows, cols)
    nblk = rows // tl
    ins = [(a, layer * nblk, 0) for a in (w, m, v)] + [(flat, d * nblk, 0) for d in range(N_DEV)]
    return _ew(fn, ins, [(cols, F32)] * 4, rows=rows, cols=cols, tl=tl, name=name,
               into=(n_layers * rows, layer * nblk, prev))


_NAMES = ['ffn1_norm', 'ffn1_w_gu', 'ffn1_w_down', 'mix_norm', 'mem_norm', 'w_mem_kv', 'xq_norm', 'xk_norm',
          'w_out', 'ffn2_norm', 'ffn2_w_gu', 'ffn2_w_down', 'sb_w_in', 's5_w_in', 's5_log_dt', 's5_a_re',
          's5_a_im', 's5_b_re', 's5_b_im', 's5_c_re', 's5_c_im', 's5_d', 's5_w_glu']
_SHARDED = {'ffn1_w_gu': 1, 'ffn1_w_down': 0, 'w_mem_kv': 0, 'w_out': 0, 'ffn2_w_gu': 1, 'ffn2_w_down': 0,
            'sb_w_in': 1, 's5_w_in': 0, 's5_w_glu': 0}
_PER_LAYER = ['ffn1_w_gu', 'ffn1_w_down', 'w_mem_kv', 'w_out', 'ffn2_w_gu', 'ffn2_w_down']
_SMALL = ['ffn1_norm', 'mix_norm', 'mem_norm', 'xq_norm', 'xk_norm', 'ffn2_norm', 's5_log_dt', 's5_a_re',
          's5_a_im', 's5_b_re', 's5_b_im', 's5_c_re', 's5_c_im']


def _ffn_fwd(x, gain, w_gu, w_down, tag, side=None, side_down=None):
    h = _rmsnorm_fwd(x, gain, f"{tag}_norm")
    gu, s, extra = _mm_swiglu(h, w_gu, f"{tag}_gu", side)
    out, extra_down = _mm(s, w_down, res=x, scale=0.5, name=f"{tag}_down", side=side_down), ()
    if side_down is not None:
        out, extra_down = out
    return out, (x, h, gu, s), extra, extra_down


def _ffn_bwd(dout, dout16, saved, gain, w_gu, w_down, tag):
    x, h, gu, s = saved
    g_down = _mm(s, dout16, ta=True, scale=0.5, out_dtype=BF16, name=f"{tag}_gdown")
    dgu, (p_down,) = _mm_dswiglu(dout16, w_down, gu, f"{tag}_ds", _scatter_side([g_down], [0]))
    g_gu = _mm(h, dgu, ta=True, out_dtype=BF16, name=f"{tag}_ggu")
    dh, (p_gu,) = _mm(dgu, w_gu, tb=True, name=f"{tag}_dh", side=_scatter_side([g_gu], [1], [True]))
    dx, dx16, g_gain = _rmsnorm_bwd(dh, x, gain, dout, f"{tag}_dnorm")
    return dx, dx16, g_gain, p_gu, p_down


def _mem_fwd(mem, mem_gain, w_kv, gk, tag):
    mem_h = _rmsnorm_fwd(mem, mem_gain, f"{tag}_memnorm")
    kv = _mm(mem_h, w_kv, name=f"{tag}_memkv")
    khat, v = _memkv_fwd(kv, gk, f"{tag}_memk")
    return mem_h, kv, khat, v


def _s5_arrange(p, j, nch):
    bt_re = jnp.transpose(p['s5_b_re'][j], (2, 0, 1))
    bt_im = jnp.transpose(p['s5_b_im'][j], (2, 0, 1))
    return bt_re, bt_im


def _mixer_fwd(i, x, p, W, mem, tag, side=None):
    L, D = x.shape
    mw = D // 4
    tok_w = D - mw
    j = i // 2
    h = _rmsnorm_fwd(x, p['mix_norm'][i], f"{tag}_norm")
    mem_h, kv, khat, v = _mem_fwd(mem, p['mem_norm'][i], W['w_mem_kv'], p['xk_norm'][i], tag)
    sv = dict(x=x, h=h, mem_h=mem_h, kv=kv, khat=khat, v=v)
    proj = _mm(h, W['sb_w_in' if i % 2 == 0 else 's5_w_in'], out_dtype=BF16 if i % 2 == 0 else F32,
               name=f"{tag}_in", side=side)
    extra = ()
    if side is not None:
        proj, extra = proj
    if i % 2 == 0:
        tok, rowsum = _sb_fwd(proj, tok_w // HEAD_DIM, f"{tag}_sb")
        sv.update(rowsum=rowsum)
    else:
        nch = tok_w // LANE
        bt_re, bt_im = _s5_arrange(p, j, nch)
        pw_re, pw_im, bb_re, bb_im = _s5_prep_fwd(p['s5_log_dt'][j], p['s5_a_re'][j], p['s5_a_im'][j],
                                                  bt_re, bt_im, f"{tag}_s5prep")
        C, N = bb_re.shape[0], bb_re.shape[2]
        bd_re = _block_diag(jnp.swapaxes(bb_re, 0, 1).reshape(nch, S5_PACK, C, N))
        bd_im = _block_diag(jnp.swapaxes(bb_im, 0, 1).reshape(nch, S5_PACK, C, N))
        ct_re = jnp.swapaxes(p['s5_c_re'][j], 1, 2).reshape(nch, S5_PACK, N, C)
        ct_im = jnp.swapaxes(p['s5_c_im'][j], 1, 2).reshape(nch, S5_PACK, N, C)
        cd_re, cd_imn = _block_diag(ct_re), _block_diag(-ct_im)
        tables = _s5_tables(pw_re, pw_im)
        y, xr, xi = _s5_fwd(proj, bd_re, bd_im, cd_re, cd_imn, tables, W['s5_d'][j], tok_w, f"{tag}_s5")
        tl = _row_tile(L, tok_w * 12)
        yg = _ew(lambda jj, yb: (_gelu(yb),), [y], [(tok_w, BF16)], rows=L, cols=tok_w, tl=tl, name=f"{tag}_gelu")[0]
        pre = _mm(yg, W['s5_w_glu'], name=f"{tag}_glu")
        tok = _ew(lambda jj, yb, pb: (_gelu(yb) * jax.nn.sigmoid(pb),), [y, pre], [(tok_w, BF16)],
                  rows=L, cols=tok_w, tl=tl, name=f"{tag}_gate")[0]
        sv.update(y=y, xr=xr, xi=xi, pre=pre, yg=yg, bt_re=bt_re, bt_im=bt_im, bd_re=bd_re, bd_im=bd_im,
                  cd_re=cd_re, cd_imn=cd_imn, tables=tables)
    cross = _memattn_fwd(proj, khat, v, p['xq_norm'][i], f"{tag}_mem")
    cat = jnp.concatenate([tok.astype(BF16), cross.astype(BF16)], axis=1)
    out = _mm(cat, W['w_out'], res=x, name=f"{tag}_out")
    sv.update(proj=proj, cat=cat)
    return out, sv, extra


def _mixer_bwd(i, dout, dout16, sv, p, W, mem, tag):
    x, h, proj, cat = sv['x'], sv['h'], sv['proj'], sv['cat']
    L, D = x.shape
    mw = D // 4
    tok_w = D - mw
    j = i // 2
    g, parts = {}, {}
    g_out = _mm(cat, dout16, ta=True, out_dtype=BF16, name=f"{tag}_gout")
    dcat, (parts['w_out'],) = _mm(dout16, W['w_out'], tb=True, name=f"{tag}_dcat",
                                  side=_scatter_side([g_out], [_SHARDED['w_out']]))
    dqm, dkhat, dv, g_xq = _memattn_bwd(proj, sv['khat'], sv['v'], p['xq_norm'][i], dcat, f"{tag}_dmem")
    g['xq_norm'] = g_xq
    dkv, g_xk = _memkv_bwd(sv['kv'], p['xk_norm'][i], dkhat, dv, f"{tag}_dmemk")
    g['xk_norm'] = g_xk
    late = {'w_mem_kv': _mm(sv['mem_h'], dkv, ta=True, out_dtype=BF16, name=f"{tag}_gmemkv")}
    dmem_h = _mm(dkv, W['w_mem_kv'], tb=True, name=f"{tag}_dmemh")
    _, _, g['mem_norm'] = _rmsnorm_bwd(dmem_h, mem, p['mem_norm'][i], None, f"{tag}_dmemnorm")
    if i % 2 == 0:
        dq, dk, dvv = _sb_bwd(proj, dcat, sv['rowsum'], tok_w // HEAD_DIM, f"{tag}_dsb")
        dproj = jnp.concatenate([dq.astype(BF16), dk.astype(BF16), dvv.astype(BF16), dqm.astype(BF16)], axis=1)
        w_in = W['sb_w_in']
        late['sb_w_in'] = _mm(h, dproj, ta=True, out_dtype=BF16, name=f"{tag}_gin")
    else:
        y, pre, yg = sv['y'], sv['pre'], sv['yg']
        tl = _row_tile(L, tok_w * 20)

        def gate_bwd(jj, dt, yb, pb):
            gl = _gelu(yb)
            sg = jax.nn.sigmoid(pb)
            return dt * gl * sg * (1.0 - sg)

        dpre = _ew(gate_bwd, [dcat, y, pre], [(tok_w, BF16)], rows=L, cols=tok_w, tl=tl, name=f"{tag}_dgate")[0]
        late['s5_w_glu'] = _mm(yg, dpre, ta=True, out_dtype=BF16, name=f"{tag}_gglu")
        dyg = _mm(dpre, W['s5_w_glu'], tb=True, name=f"{tag}_dyg")

        def gelu_bwd(jj, dt, yb, pb, db):
            return (dt * jax.nn.sigmoid(pb) + db) * _gelu_grad(yb)

        dy = _ew(gelu_bwd, [dcat, y, pre, dyg], [(tok_w, F32)], rows=L, cols=tok_w, tl=tl, name=f"{tag}_dgelu")[0]
        du, gcd_re, gcd_imn, gbd_re, gbd_im, ga_re, ga_im, g_d = _s5_bwd(
            proj, dy, sv['xr'], sv['xi'], sv['bd_re'], sv['bd_im'], sv['cd_re'], sv['cd_imn'], sv['tables'],
            W['s5_d'][j], tok_w, f"{tag}_ds5")
        G, N = p['s5_a_re'].shape[1], p['s5_a_re'].shape[2]
        C = S5_GROUP
        gct_re = _block_diag_extract(gcd_re, S5_PACK).reshape(G, N, C)
        gct_im = -_block_diag_extract(gcd_imn, S5_PACK).reshape(G, N, C)
        g['s5_c_re'] = jnp.swapaxes(gct_re, 1, 2)
        g['s5_c_im'] = jnp.swapaxes(gct_im, 1, 2)
        gbb_re = jnp.swapaxes(_block_diag_extract(gbd_re, S5_PACK).reshape(G, C, N), 0, 1)
        gbb_im = jnp.swapaxes(_block_diag_extract(gbd_im, S5_PACK).reshape(G, C, N), 0, 1)
        g_ld, g_are, g_aim, gbt_re, gbt_im = _s5_prep_bwd(
            p['s5_log_dt'][j], p['s5_a_re'][j], p['s5_a_im'][j], sv['bt_re'], sv['bt_im'],
            ga_re.reshape(G, N), ga_im.reshape(G, N), gbb_re, gbb_im, f"{tag}_ds5prep")
        g['s5_log_dt'] = g_ld.reshape(G)
        g['s5_a_re'], g['s5_a_im'] = g_are, g_aim
        g['s5_b_re'], g['s5_b_im'] = jnp.transpose(gbt_re, (1, 2, 0)), jnp.transpose(gbt_im, (1, 2, 0))
        g['s5_d'] = g_d.reshape(tok_w)
        dproj = jnp.concatenate([du.astype(BF16), dqm.astype(BF16)], axis=1)
        w_in = W['s5_w_in']
        late['s5_w_in'] = _mm(h, dproj, ta=True, out_dtype=BF16, name=f"{tag}_gin")
    names = list(late)
    dh, bufs = _mm(dproj, w_in, tb=True, name=f"{tag}_dh",
                   side=_scatter_side([late[n] for n in names], [_SHARDED[n] for n in names]))
    parts.update(zip(names, bufs))
    dx, dx16, g['mix_norm'] = _rmsnorm_bwd(dh, x, p['mix_norm'][i], dout, f"{tag}_dnorm")
    return dx, dx16, g, parts


def _pack(arrs):
    flat = jnp.concatenate([a.reshape(-1).astype(F32) for a in arrs])
    pad = (-flat.shape[0]) % (512 * LANE)
    flat = jnp.pad(flat, (0, pad))
    return flat.reshape(-1, LANE)


def _unpack(flat, like):
    flat = flat.reshape(-1)
    out, off = [], 0
    for a in like:
        n = math.prod(a.shape)
        out.append(flat[off:off + n].reshape(a.shape))
        off += n
    return out


def kernel(x, mem, ffn1_norm, ffn1_w_gu, ffn1_w_down, mix_norm, mem_norm, w_mem_kv, xq_norm, xk_norm, w_out, ffn2_norm, ffn2_w_gu, ffn2_w_down, sb_w_in, s5_w_in, s5_log_dt, s5_a_re, s5_a_im, s5_b_re, s5_b_im, s5_c_re, s5_c_im, s5_d, s5_w_glu, loss_target, m_ffn1_norm, m_ffn1_w_gu, m_ffn1_w_down, m_mix_norm, m_mem_norm, m_w_mem_kv, m_xq_norm, m_xk_norm, m_w_out, m_ffn2_norm, m_ffn2_w_gu, m_ffn2_w_down, m_sb_w_in, m_s5_w_in, m_s5_log_dt, m_s5_a_re, m_s5_a_im, m_s5_b_re, m_s5_b_im, m_s5_c_re, m_s5_c_im, m_s5_d, m_s5_w_glu, v_ffn1_norm, v_ffn1_w_gu, v_ffn1_w_down, v_mix_norm, v_mem_norm, v_w_mem_kv, v_xq_norm, v_xk_norm, v_w_out, v_ffn2_norm, v_ffn2_w_gu, v_ffn2_w_down, v_sb_w_in, v_s5_w_in, v_s5_log_dt, v_s5_a_re, v_s5_a_im, v_s5_b_re, v_s5_b_im, v_s5_c_re, v_s5_c_im, v_s5_d, v_s5_w_glu):
    given = dict(locals())
    p = {n: given[n] for n in _NAMES}
    mom = {n: given["m_" + n] for n in _NAMES}
    var = {n: given["v_" + n] for n in _NAMES}
    depth = ffn1_norm.shape[0]
    xs, mems, target = x[0], mem[0], loss_target[0]
    L, D = xs.shape
    me = 4 * lax.axis_index("x") + 2 * lax.axis_index("y") + lax.axis_index("c")

    def layer_names(i):
        return _PER_LAYER + (['sb_w_in'] if i % 2 == 0 else ['s5_w_in', 's5_w_glu'])

    n_b = s5_d.shape[0]
    d_rows = jnp.broadcast_to(s5_d.reshape(1, -1), (8, s5_d.size))
    d_full = _all_gather([d_rows], [0], "gather_s5d")[0][::8]
    d_full = jnp.swapaxes(d_full.reshape(N_DEV, n_b, -1), 0, 1).reshape(n_b, -1)

    def gather_group(i, names):
        shards = [_cast_bf16(p[nm], i // 2 if nm in ('sb_w_in', 's5_w_in', 's5_w_glu') else i, f"cast_{nm}")
                  for nm in names]
        return _gather_side(shards, [_SHARDED[nm] for nm in names], [nm.endswith('w_gu') for nm in names])

    groups = lambda i: (['ffn1_w_gu', 'ffn1_w_down'], [nm for nm in layer_names(i) if 'ffn' not in nm],
                        ['ffn2_w_gu', 'ffn2_w_down'])
    first, rest = groups(0)[0], groups(0)[1] + groups(0)[2]
    W = dict(zip(first, _run_side(gather_group(0, first), "gather_first")))
    Ws, saves = [], []
    act = xs
    for i in range(depth):
        W['s5_d'] = d_full
        Ws.append(W)
        nxt = [gather_group(i + 1, names) if i + 1 < depth else None for names in groups(i + 1)]
        if i == 0:
            act, s1, got, e1 = _ffn_fwd(act, p['ffn1_norm'][i], W['ffn1_w_gu'], W['ffn1_w_down'], "ffn1",
                                        gather_group(0, rest), nxt[0])
            W.update(zip(rest, got))
        else:
            act, s1, e1, _ = _ffn_fwd(act, p['ffn1_norm'][i], W['ffn1_w_gu'], W['ffn1_w_down'], "ffn1", nxt[0])
        act, s2, e2 = _mixer_fwd(i, act, p, W, mems, f"mix{i % 2}", nxt[1])
        act, s3, e3, _ = _ffn_fwd(act, p['ffn2_norm'][i], W['ffn2_w_gu'], W['ffn2_w_down'], "ffn2", nxt[2])
        saves.append((s1, s2, s3))
        W = {}
        if i + 1 < depth:
            for names, got in zip(groups(i + 1), (e1, e2, e3)):
                W.update(zip(names, got))

    loss_part, dact, dact16 = _loss_fwd_bwd(act, target, "loss")
    loss = lax.psum(loss_part[0, 0], ("x", "y", "c"))

    small_g = {n: [None] * p[n].shape[0] for n in _SMALL + ['s5_d']}
    big = {n: None for n in _SHARDED}
    for i in reversed(range(depth)):
        W = Ws[i]
        s1, s2, s3 = saves[i]
        j = i // 2
        parts = {}
        dact, dact16, gg, parts['ffn2_w_gu'], parts['ffn2_w_down'] = _ffn_bwd(
            dact, dact16, s3, p['ffn2_norm'][i], W['ffn2_w_gu'], W['ffn2_w_down'], "ffn2")
        small_g['ffn2_norm'][i] = gg
        dact, dact16, gm, pm = _mixer_bwd(i, dact, dact16, s2, p, W, mems, f"mix{i % 2}")
        parts.update(pm)
        dact, dact16, gg, parts['ffn1_w_gu'], parts['ffn1_w_down'] = _ffn_bwd(
            dact, dact16, s1, p['ffn1_norm'][i], W['ffn1_w_gu'], W['ffn1_w_down'], "ffn1")
        small_g['ffn1_norm'][i] = gg
        for n in _SMALL + ['s5_d']:
            if n in gm:
                small_g[n][j if n.startswith('s5_') else i] = gm[n]
        for nm in layer_names(i):
            li = j if nm in ('sb_w_in', 's5_w_in', 's5_w_glu') else i
            flat2 = lambda a: a.reshape(-1, a.shape[-1])
            big[nm] = _adamw_sum(flat2(p[nm]), flat2(mom[nm]), flat2(var[nm]), parts[nm], f"adamw_{nm}",
                                 layer=li, prev=big[nm])

    small_names = _SMALL + ['s5_d']
    g_small = [jnp.stack([gi.reshape(p[n].shape[1:]) if n != 's5_d' else gi for gi in small_g[n]])
               for n in small_names]
    d_w = jnp.zeros_like(d_full)
    cw = s5_d.shape[1]
    w_small = [p[n] for n in _SMALL]
    m_small = [mom[n] for n in _SMALL]
    v_small = [var[n] for n in _SMALL]
    place = lambda a: lax.dynamic_update_slice(d_w, a, (0, me * cw))
    flat_g = _pack(g_small)
    flat_w = _pack(w_small + [place(s5_d)])
    flat_m = _pack(m_small + [place(mom['s5_d'])])
    flat_v = _pack(v_small + [place(var['s5_d'])])
    parts = _all_gather([flat_g.reshape((1,) + flat_g.shape)], [0], "gather_small")[0]
    res_small = _adamw_sum(flat_w, flat_m, flat_v, parts, "adamw_small")
    like = [p[n] for n in _SMALL] + [d_full]
    un = [_unpack(r, like) for r in res_small]

    def result(kind, n):
        if n in _SHARDED:
            return big[n][kind].reshape(p[n].shape)
        if n == 's5_d':
            return lax.dynamic_slice(un[kind][-1], (0, me * cw), (n_b, cw))
        return un[kind][_SMALL.index(n)]

    outs = [loss, dact.reshape((1,) + dact.shape)]
    for kind in range(4):
        outs += [result(kind, n) for n in _NAMES]
    return tuple(outs)
```

```python
import functools
import math

import jax
import jax.numpy as jnp
from jax import lax
from jax.experimental import pallas as pl
from jax.experimental.pallas import tpu as pltpu

F32 = jnp.float32
BF16 = jnp.bfloat16
MESH = pl.DeviceIdType.MESH

HEAD_DIM = 128
S5_GROUP = 16
S5_STATE = 64
S5_PACK = 8
EPS = 1e-6
ADAM_LR, ADAM_B1, ADAM_B2, ADAM_EPS, ADAM_WD, ADAM_STEP = 0.001, 0.9, 0.999, 1e-08, 0.01, 10
N_DEV = 8
VMEM_LIMIT_BYTES = 56 * 1024 * 1024
SB_TILE = 256
LANE = 128


def _params(*sem):
    if sem:
        return pltpu.CompilerParams(dimension_semantics=sem, vmem_limit_bytes=VMEM_LIMIT_BYTES)
    return pltpu.CompilerParams(vmem_limit_bytes=VMEM_LIMIT_BYTES)


def _pick(dim, target, align=LANE):
    if dim <= target:
        return dim
    best = None
    for t in range(align, target + 1, align):
        if dim % t == 0:
            best = t
    assert best is not None, (dim, target, align)
    return best


MM_OPERAND_BYTES = 20 * 1024 * 1024
ANY_SPEC = pl.BlockSpec(memory_space=pl.ANY)


class _Side:
    def __init__(self, ins, out_shape, scratch, start, finish):
        self.ins, self.out_shape, self.scratch, self.start, self.finish = ins, out_shape, scratch, start, finish


def _with_side(side, n_in, n_out, n_scratch, refs, first, last, compute):
    if side is None:
        compute(refs)
        return
    ns_in, ns_out = len(side.ins), len(side.out_shape)
    own_in, s_in = refs[:n_in], refs[n_in:n_in + ns_in]
    o0 = n_in + ns_in
    own_out, s_out = refs[o0:o0 + n_out], refs[o0 + n_out:o0 + n_out + ns_out]
    c0 = o0 + n_out + ns_out
    own_scr, s_scr = refs[c0:c0 + n_scratch], refs[c0 + n_scratch:]

    @pl.when(first)
    def _():
        side.start(s_in, s_out, s_scr)

    compute(tuple(own_in) + tuple(own_out) + tuple(own_scr))

    @pl.when(last)
    def _():
        side.finish(s_in, s_out, s_scr)


def _side_call(body_own, side, *, name, grid, in_specs, out_specs, out_shape, scratch, args, sem):
    n_in, n_out, n_scr = len(args), len(out_shape), len(scratch)
    if side is not None:
        in_specs = list(in_specs) + [ANY_SPEC] * len(side.ins)
        out_specs = list(out_specs) + [ANY_SPEC] * len(side.out_shape)
        out_shape = list(out_shape) + list(side.out_shape)
        scratch = list(scratch) + list(side.scratch)
        args = list(args) + list(side.ins)
        sem = ("arbitrary",) * len(grid)

    def body(*refs):
        ids = [pl.program_id(d) for d in range(len(grid))]
        first, last = ids[0] == 0, ids[0] == grid[0] - 1
        for d in range(1, len(grid)):
            first = jnp.logical_and(first, ids[d] == 0)
            last = jnp.logical_and(last, ids[d] == grid[d] - 1)
        _with_side(side, n_in, n_out, n_scr, refs, first, last, body_own)

    res = pl.pallas_call(body, name=name, grid=grid, in_specs=in_specs, out_specs=out_specs,
                         out_shape=out_shape, scratch_shapes=scratch, compiler_params=_params(*sem))(*args)
    return res[:n_out], res[n_out:]


def _run_side(side, name):
    def body(*refs):
        ni, no = len(side.ins), len(side.out_shape)
        side.start(refs[:ni], refs[ni:ni + no], refs[ni + no:])
        side.finish(refs[:ni], refs[ni:ni + no], refs[ni + no:])

    return pl.pallas_call(body, name=name, in_specs=[ANY_SPEC] * len(side.ins),
                          out_specs=[ANY_SPEC] * len(side.out_shape), out_shape=list(side.out_shape),
                          scratch_shapes=list(side.scratch))(*side.ins)


def _mm(a, b, *, ta=False, tb=False, out_dtype=F32, res=None, scale=None, name, side=None):
    if ta:
        K, M = a.shape
    else:
        M, K = a.shape
    if tb:
        N, Kb = b.shape
    else:
        Kb, N = b.shape
    assert K == Kb, (a.shape, b.shape, ta, tb)
    tm, tn = _pick(M, 1024), _pick(N, 1024)
    per_k = 2 * (tm * a.dtype.itemsize + tn * b.dtype.itemsize)
    tk = _pick(K, max(LANE, MM_OPERAND_BYTES // per_k))
    nk = K // tk
    dims = (((0 if ta else 1,), (1 if tb else 0,)), ((), ()))

    def finish(o, r_ref, o_ref):
        if scale is not None:
            o = o * scale
        if r_ref is not None:
            o = r_ref[...].astype(F32) + o
        o_ref[...] = o.astype(o_ref.dtype)

    def compute(refs):
        a_ref, b_ref = refs[0], refs[1]
        r_ref = refs[2] if res is not None else None
        o_ref = refs[3] if res is not None else refs[2]
        part = lax.dot_general(a_ref[...].astype(BF16), b_ref[...].astype(BF16), dims, preferred_element_type=F32)
        if nk == 1:
            finish(part, r_ref, o_ref)
            return
        acc = refs[-1]
        k = pl.program_id(2)

        @pl.when(k == 0)
        def _():
            acc[...] = part

        @pl.when(k > 0)
        def _():
            acc[...] += part

        @pl.when(k == nk - 1)
        def _():
            finish(acc[...], r_ref, o_ref)

    a_spec = pl.BlockSpec((tk, tm), lambda i, j, k: (k, i)) if ta else pl.BlockSpec((tm, tk), lambda i, j, k: (i, k))
    b_spec = pl.BlockSpec((tn, tk), lambda i, j, k: (j, k)) if tb else pl.BlockSpec((tk, tn), lambda i, j, k: (k, j))
    o_spec = pl.BlockSpec((tm, tn), lambda i, j, k: (i, j))
    in_specs, args = [a_spec, b_spec], [a, b]
    if res is not None:
        in_specs.append(o_spec)
        args.append(res)
    own, extra = _side_call(
        compute, side, name=name, grid=(M // tm, N // tn, nk), in_specs=in_specs, out_specs=[o_spec],
        out_shape=[jax.ShapeDtypeStruct((M, N), out_dtype)],
        scratch=[pltpu.VMEM((tm, tn), F32)] if nk > 1 else [], args=args,
        sem=("parallel", "parallel", "arbitrary"))
    return own[0] if side is None else (own[0], extra)


def _col_chunks(width, step=2 * LANE):
    return [(lo, min(lo + step, width)) for lo in range(0, width, step)]


def _silu_grads(g):
    sg = jax.nn.sigmoid(g)
    return g * sg, sg * (1.0 + g * (1.0 - sg))


def _mm_swiglu(h, w_gu, name, side=None):
    M, K = h.shape
    n2 = w_gu.shape[1]
    c = n2 // N_DEV
    tm = _pick(M, 512)

    def compute(refs):
        h_ref, w_ref, gu_ref, s_ref = refs
        r = jnp.dot(h_ref[...], w_ref[...], preferred_element_type=F32)
        gu_ref[...] = r.astype(gu_ref.dtype)
        act, _ = _silu_grads(r[:, :c])
        s_ref[...] = (act * r[:, c:]).astype(s_ref.dtype)

    own, extra = _side_call(
        compute, side, name=name, grid=(N_DEV // 2, M // tm),
        in_specs=[pl.BlockSpec((tm, K), lambda j, i: (i, 0)), pl.BlockSpec((K, 2 * c), lambda j, i: (0, j))],
        out_specs=[pl.BlockSpec((tm, 2 * c), lambda j, i: (i, j)), pl.BlockSpec((tm, c), lambda j, i: (i, j))],
        out_shape=[jax.ShapeDtypeStruct((M, n2), BF16), jax.ShapeDtypeStruct((M, n2 // 2), BF16)],
        scratch=[], args=[h, w_gu], sem=("parallel", "parallel"))
    return own[0], own[1], extra


def _mm_dswiglu(dout, w_down, gu, name, side=None):
    M, D = dout.shape
    F_ = w_down.shape[0]
    c = F_ // (N_DEV // 2)
    tm = _pick(M, 512)
    nt = (((1,), (1,)), ((), ()))

    def compute(refs):
        d_ref, w_ref, gu_ref, o_ref = refs
        d = d_ref[...]
        for lo, hi in _col_chunks(c):
            ds = 0.5 * lax.dot_general(d, w_ref[lo:hi, :], nt, preferred_element_type=F32)
            g, u = gu_ref[:, lo:hi].astype(F32), gu_ref[:, c + lo:c + hi].astype(F32)
            act, dact = _silu_grads(g)
            o_ref[:, lo:hi] = (ds * u * dact).astype(o_ref.dtype)
            o_ref[:, c + lo:c + hi] = (ds * act).astype(o_ref.dtype)

    own, extra = _side_call(
        compute, side, name=name, grid=(N_DEV // 2, M // tm),
        in_specs=[pl.BlockSpec((tm, D), lambda j, i: (i, 0)), pl.BlockSpec((c, D), lambda j, i: (j, 0)),
                  pl.BlockSpec((tm, 2 * c), lambda j, i: (i, j))],
        out_specs=[pl.BlockSpec((tm, 2 * c), lambda j, i: (i, j))],
        out_shape=[jax.ShapeDtypeStruct((M, 2 * F_), BF16)],
        scratch=[], args=[dout, w_down, gu], sem=("parallel", "parallel"))
    return own[0], extra


class _Whole:
    def __init__(self, a):
        self.a = a


def _ew(fn, ins, outs, accs=(), *, rows, cols, tl, tc=None, name, into=None):
    tc = cols if tc is None else tc
    nj, ni = cols // tc, rows // tl
    assert nj * tc == cols and ni * tl == rows, (rows, cols, tl, tc)
    in_specs, args = [], []
    for it in ins:
        if isinstance(it, _Whole):
            in_specs.append(pl.BlockSpec(it.a.shape, lambda j, i, _n=it.a.ndim: (0,) * _n))
            args.append(it.a)
        else:
            arr, ro, co = it if isinstance(it, tuple) else (it, 0, 0)
            in_specs.append(pl.BlockSpec((tl, tc), lambda j, i, _ro=ro, _co=co: (i + _ro, j + _co)))
            args.append(arr)
    n_in, n_out = len(args), len(outs)
    out_rows, out_off, prev = (rows, 0, None) if into is None else into
    out_shape = [jax.ShapeDtypeStruct((out_rows, c), dt) for c, dt in outs]
    out_specs = [pl.BlockSpec((tl, tc), lambda j, i: (i + out_off, j)) for _ in outs]
    for c, t in accs:
        out_shape.append(jax.ShapeDtypeStruct((1, c), F32))
        out_specs.append(pl.BlockSpec((1, t), lambda j, i: (0, j)))
    aliases = {}
    if prev is not None:
        in_specs += [ANY_SPEC] * n_out
        args += list(prev)
        aliases = {n_in + k: k for k in range(n_out)}
    n_args = len(args)

    def body(*refs):
        j, i = pl.program_id(0), pl.program_id(1)
        vals = fn(j, *[r[...] for r in refs[:n_in]])
        if not isinstance(vals, (tuple, list)):
            vals = (vals,)
        for r, v in zip(refs[n_args:n_args + n_out], vals[:n_out]):
            r[...] = v.astype(r.dtype)
        for r, v in zip(refs[n_args + n_out:], vals[n_out:]):
            @pl.when(i == 0)
            def _(r=r):
                r[...] = jnp.zeros_like(r)
            r[...] += v

    res = pl.pallas_call(
        body, name=name, grid=(nj, ni), in_specs=in_specs, out_specs=out_specs, out_shape=out_shape,
        input_output_aliases=aliases, compiler_params=_params("parallel", "arbitrary"),
    )(*args)
    return res


def _row_tile(rows, bytes_per_row, budget=10 * 1024 * 1024, align=16):
    cap = max(align, budget // max(1, bytes_per_row))
    best = None
    for t in range(align, min(rows, cap) + 1, align):
        if rows % t == 0:
            best = t
    if best is None:
        best = rows
    return best


def _cast_bf16(w, layer, name):
    _, rows, cols = w.shape
    tl = _row_tile(rows, cols * 6)
    return _ew(lambda j, b: b, [(w.reshape(-1, cols), layer * (rows // tl), 0)], [(cols, BF16)],
               rows=rows, cols=cols, tl=tl, name=name)[0]


def _rmsnorm_fwd(x, g, name, out_dtype=BF16):
    rows, cols = x.shape

    def fn(j, xb, gb):
        r = lax.rsqrt(jnp.mean(xb * xb, axis=-1, keepdims=True) + EPS)
        return (xb * r * gb,)

    tl = _row_tile(rows, cols * 12)
    return _ew(fn, [x, _Whole(g.reshape(1, cols))], [(cols, out_dtype)], rows=rows, cols=cols, tl=tl, name=name)[0]


def _rmsnorm_bwd(dh, x, g, dres, name):
    rows, cols = x.shape

    def fn(j, dhb, xb, gb, *rest):
        r = lax.rsqrt(jnp.mean(xb * xb, axis=-1, keepdims=True) + EPS)
        xh = xb * r
        dxh = dhb * gb
        dx = r * (dxh - xh * jnp.mean(dxh * xh, axis=-1, keepdims=True))
        if rest:
            dx = dx + rest[0]
        return dx, dx, jnp.sum(dhb * xh, axis=0, keepdims=True)

    ins = [dh, x, _Whole(g.reshape(1, cols))] + ([dres] if dres is not None else [])
    tl = _row_tile(rows, cols * 28)
    return _ew(fn, ins, [(cols, F32), (cols, BF16)], [(cols, cols)], rows=rows, cols=cols, tl=tl, name=name)


_GELU_K = math.sqrt(2.0 / math.pi)
_GELU_C = 0.044715


def _gelu(y):
    return 0.5 * y * (1.0 + jnp.tanh(_GELU_K * (y + _GELU_C * y * y * y)))


def _gelu_grad(y):
    t = jnp.tanh(_GELU_K * (y + _GELU_C * y * y * y))
    return 0.5 * (1.0 + t) + 0.5 * y * (1.0 - t * t) * _GELU_K * (1.0 + 3.0 * _GELU_C * y * y)


def _loss_fwd_bwd(y, target, name):
    rows, cols = y.shape

    def fn(j, yb, tb):
        e = yb - tb
        part = 0.5 * jnp.sum(e * e) / cols
        dy = e * (1.0 / cols)
        return dy, dy, jnp.full((1, LANE), part, F32)

    tl = _row_tile(rows, cols * 20)
    dy, dy16, loss = _ew(fn, [y, target], [(cols, F32), (cols, BF16)], [(LANE, LANE)], rows=rows, cols=cols,
                         tl=tl, name=name)
    return loss, dy, dy16


SB_SCALE = 1.0 / math.sqrt(HEAD_DIM)
SB_ROWS = 4


def _sb_terms(z):
    z2 = z * (SB_SCALE / math.log(2.0))
    t = jnp.log2(1.0 + jnp.exp2(-jnp.abs(z2)))
    return jnp.maximum(z2, 0.0) + t, jnp.minimum(z2, 0.0) - t


def _strict_lower(n, m):
    return lax.broadcasted_iota(jnp.int32, (n, m), 0) > lax.broadcasted_iota(jnp.int32, (n, m), 1)


def _sb_tiles(L):
    T = min(SB_TILE, L)
    R = max(r for r in (SB_ROWS, 2, 1) if L % (r * T) == 0)
    return T, R


def _sb_fwd(proj, n_heads, name):
    L = proj.shape[0]
    T, R = _sb_tiles(L)
    M = R * T
    H = n_heads
    nt = (((1,), (1,)), ((), ()))

    def body(q_ref, k_ref, v_ref, o_ref, s_ref, acc_ref, later_ref):
        i = pl.program_id(1)
        acc_ref[...] = jnp.zeros_like(acc_ref)
        later_ref[...] = jnp.zeros_like(later_ref)
        upper = _strict_lower(T, T).astype(BF16)

        def strip(row0, nrows, jb, diagonal):
            rows = slice(row0, row0 + nrows)
            start = pl.multiple_of(jb * T, T)
            kb, vb = k_ref[pl.ds(start, T), :], v_ref[pl.ds(start, T), :]
            z = lax.dot_general(q_ref[rows, :], kb, nt, preferred_element_type=F32)
            sp, logsig = _sb_terms(z)
            mask = _strict_lower(nrows, T) if diagonal else None
            spb = (jnp.where(mask, sp, 0.0) if diagonal else sp).astype(BF16)
            between = jnp.dot(spb, upper, preferred_element_type=F32)
            later = later_ref[rows, :]
            w = jnp.exp2(logsig - between - later)
            if diagonal:
                w = jnp.where(mask, w, 0.0)
            acc_ref[rows, :] += jnp.dot(w.astype(BF16), vb, preferred_element_type=F32)
            later_ref[rows, :] = later + between[:, 0:1] + spb[:, 0:1].astype(F32)

        for d in reversed(range(R)):
            strip(d * T, M - d * T, R * i + d, True)

        def step(it, carry):
            strip(0, M, R * i - 1 - it, False)
            return carry

        lax.fori_loop(0, R * i, step, 0)
        o_ref[...] = acc_ref[...].astype(o_ref.dtype)
        s_ref[...] = jnp.broadcast_to(later_ref[...], (M, HEAD_DIM))

    return pl.pallas_call(
        body, name=name, grid=(H, L // M),
        in_specs=[pl.BlockSpec((M, HEAD_DIM), lambda h, i: (i, h)),
                  pl.BlockSpec((L, HEAD_DIM), lambda h, i: (0, H + h)),
                  pl.BlockSpec((L, HEAD_DIM), lambda h, i: (0, 2 * H + h))],
        out_specs=[pl.BlockSpec((M, HEAD_DIM), lambda h, i: (i, h)),
                   pl.BlockSpec((M, HEAD_DIM), lambda h, i: (i, h))],
        out_shape=[jax.ShapeDtypeStruct((L, H * HEAD_DIM), BF16), jax.ShapeDtypeStruct((L, H * HEAD_DIM), F32)],
        scratch_shapes=[pltpu.VMEM((M, HEAD_DIM), F32), pltpu.VMEM((M, 1), F32)],
        compiler_params=_params("parallel", "arbitrary"),
    )(proj, proj, proj)


def _sb_bwd(proj, do, rowsum, n_heads, name):
    L = proj.shape[0]
    T, R = _sb_tiles(L)
    M = R * T
    H = n_heads
    nt = (((1,), (1,)), ((), ()))
    tn = (((0,), (0,)), ((), ()))

    def body(q_ref, k_ref, v_ref, do_ref, s_ref, dq_out, dk_out, dv_out,
             do16_ref, upto_ref, hsum_ref, dq_ref, dk_ref, dv_ref):
        i = pl.program_id(1)

        @pl.when(i == 0)
        def _():
            dk_ref[...] = jnp.zeros_like(dk_ref)
            dv_ref[...] = jnp.zeros_like(dv_ref)

        dq_ref[...] = jnp.zeros_like(dq_ref)
        upto_ref[...] = jnp.zeros_like(upto_ref)
        hsum_ref[...] = jnp.zeros_like(hsum_ref)
        do16_ref[...] = do_ref[...].astype(BF16)
        upper = _strict_lower(T, T).astype(BF16)
        lower_inc = jnp.logical_not(_strict_lower(T, T)).astype(BF16)

        def strip(row0, nrows, jb, diagonal):
            rows = slice(row0, row0 + nrows)
            start = pl.multiple_of(jb * T, T)
            kb, vb = k_ref[pl.ds(start, T), :], v_ref[pl.ds(start, T), :]
            q, dob = q_ref[rows, :], do16_ref[rows, :]
            z = lax.dot_general(q, kb, nt, preferred_element_type=F32)
            sp, logsig = _sb_terms(z)
            mask = _strict_lower(nrows, T) if diagonal else None
            spb = (jnp.where(mask, sp, 0.0) if diagonal else sp).astype(BF16)
            between = jnp.dot(spb, upper, preferred_element_type=F32)
            upto = upto_ref[rows, :] + between[:, 0:1] + spb[:, 0:1].astype(F32)
            upto_ref[rows, :] = upto
            a = jnp.exp2(logsig - between - (s_ref[rows, 0:1] - upto))
            if diagonal:
                a = jnp.where(mask, a, 0.0)
            g = a * lax.dot_general(dob, vb, nt, preferred_element_type=F32)
            hcum = jnp.dot(g.astype(BF16), lower_inc, preferred_element_type=F32) + hsum_ref[rows, :]
            hsum_ref[rows, :] = hcum[:, T - 1:T]
            dz = (g - jnp.exp2(logsig) * hcum) * SB_SCALE
            if diagonal:
                dz = jnp.where(mask, dz, 0.0)
            dzb, ab = dz.astype(BF16), a.astype(BF16)
            dq_ref[rows, :] += jnp.dot(dzb, kb, preferred_element_type=F32)
            dk_ref[pl.ds(start, T), :] += lax.dot_general(dzb, q, tn, preferred_element_type=F32)
            dv_ref[pl.ds(start, T), :] += lax.dot_general(ab, dob, tn, preferred_element_type=F32)

        def step(jb, carry):
            strip(0, M, jb, False)
            return carry

        lax.fori_loop(0, R * i, step, 0)
        for d in range(R):
            strip(d * T, M - d * T, R * i + d, True)
        dq_out[...] = dq_ref[...].astype(dq_out.dtype)

        @pl.when(i == L // M - 1)
        def _():
            dk_out[...] = dk_ref[...].astype(dk_out.dtype)
            dv_out[...] = dv_ref[...].astype(dv_out.dtype)

    shp = jax.ShapeDtypeStruct((L, H * HEAD_DIM), BF16)
    return pl.pallas_call(
        body, name=name, grid=(H, L // M),
        in_specs=[pl.BlockSpec((M, HEAD_DIM), lambda h, i: (i, h)),
                  pl.BlockSpec((L, HEAD_DIM), lambda h, i: (0, H + h)),
                  pl.BlockSpec((L, HEAD_DIM), lambda h, i: (0, 2 * H + h)),
                  pl.BlockSpec((M, HEAD_DIM), lambda h, i: (i, h)),
                  pl.BlockSpec((M, HEAD_DIM), lambda h, i: (i, h))],
        out_specs=[pl.BlockSpec((M, HEAD_DIM), lambda h, i: (i, h)),
                   pl.BlockSpec((L, HEAD_DIM), lambda h, i: (0, h)),
                   pl.BlockSpec((L, HEAD_DIM), lambda h, i: (0, h))],
        out_shape=[shp, shp, shp],
        scratch_shapes=[pltpu.VMEM((M, HEAD_DIM), BF16), pltpu.VMEM((M, 1), F32), pltpu.VMEM((M, 1), F32),
                        pltpu.VMEM((M, HEAD_DIM), F32), pltpu.VMEM((L, HEAD_DIM), F32),
                        pltpu.VMEM((L, HEAD_DIM), F32)],
        compiler_params=_params("parallel", "arbitrary"),
    )(proj, proj, proj, do, rowsum)


def _head_norm(v):
    r = lax.rsqrt(jnp.mean(v * v, axis=-1, keepdims=True) + EPS)
    return v * r, r


def _memkv_fwd(kv, gk, name):
    M, w2 = kv.shape
    mw = w2 // 2

    def body(kv_ref, gk_ref, k_ref, v_ref):
        for h in range(mw // HEAD_DIM):
            sl = slice(h * HEAD_DIM, (h + 1) * HEAD_DIM)
            xh, _ = _head_norm(kv_ref[:, sl])
            k_ref[:, sl] = xh * gk_ref[...]
        v_ref[...] = kv_ref[:, mw:]

    return pl.pallas_call(body, name=name, out_shape=[jax.ShapeDtypeStruct((M, mw), F32)] * 2,
                          compiler_params=_params())(kv, gk.reshape(1, HEAD_DIM))


def _memkv_bwd(kv, gk, dk, dv, name):
    M, w2 = kv.shape
    mw = w2 // 2

    def body(kv_ref, gk_ref, dk_ref, dv_ref, dkv_ref, dg_ref):
        dg = jnp.zeros((1, HEAD_DIM), F32)
        for h in range(mw // HEAD_DIM):
            sl = slice(h * HEAD_DIM, (h + 1) * HEAD_DIM)
            xh, r = _head_norm(kv_ref[:, sl])
            d = dk_ref[:, sl]
            dg = dg + jnp.sum(d * xh, axis=0, keepdims=True)
            dxh = d * gk_ref[...]
            dkv_ref[:, sl] = r * (dxh - xh * jnp.mean(dxh * xh, axis=-1, keepdims=True))
        dkv_ref[:, mw:] = dv_ref[...]
        dg_ref[...] = dg

    return pl.pallas_call(body, name=name,
                          out_shape=[jax.ShapeDtypeStruct((M, w2), F32), jax.ShapeDtypeStruct((1, HEAD_DIM), F32)],
                          compiler_params=_params())(kv, gk.reshape(1, HEAD_DIM), dk, dv)


def _mem_tile(L):
    return _pick(L, 512, 8)


def _memattn_fwd(proj, khat, v, gq, name):
    L, W = proj.shape
    M, mw = khat.shape
    assert (W - mw) % mw == 0
    qcol = (W - mw) // mw
    tl = _mem_tile(L)
    scale = 1.0 / math.sqrt(HEAD_DIM)

    def body(q_ref, k_ref, v_ref, g_ref, o_ref):
        for h in range(mw // HEAD_DIM):
            sl = slice(h * HEAD_DIM, (h + 1) * HEAD_DIM)
            xh, _ = _head_norm(q_ref[:, sl].astype(F32))
            qh = (xh * g_ref[...]).astype(BF16)
            s = lax.dot_general(qh, k_ref[:, sl].astype(BF16), (((1,), (1,)), ((), ())),
                                preferred_element_type=F32) * scale
            e = jnp.exp(s - jnp.max(s, axis=-1, keepdims=True))
            p = e / jnp.sum(e, axis=-1, keepdims=True)
            o_ref[:, sl] = jnp.dot(p.astype(BF16), v_ref[:, sl].astype(BF16),
                                   preferred_element_type=F32).astype(o_ref.dtype)

    whole = lambda a: pl.BlockSpec(a.shape, lambda i: (0,) * a.ndim)
    g2 = gq.reshape(1, HEAD_DIM)
    return pl.pallas_call(
        body, name=name, grid=(L // tl,),
        in_specs=[pl.BlockSpec((tl, mw), lambda i: (i, qcol)), whole(khat), whole(v), whole(g2)],
        out_specs=pl.BlockSpec((tl, mw), lambda i: (i, 0)),
        out_shape=jax.ShapeDtypeStruct((L, mw), BF16),
        compiler_params=_params("parallel"),
    )(proj, khat, v, g2)


def _memattn_bwd(proj, khat, v, gq, dcat, name):
    L, W = proj.shape
    M, mw = khat.shape
    qcol = (W - mw) // mw
    dcol = (dcat.shape[1] - mw) // mw
    tl = _mem_tile(L)
    scale = 1.0 / math.sqrt(HEAD_DIM)

    def body(q_ref, k_ref, v_ref, g_ref, do_ref, dq_ref, dk_ref, dv_ref, dg_ref):
        i = pl.program_id(0)

        @pl.when(i == 0)
        def _():
            dk_ref[...] = jnp.zeros_like(dk_ref)
            dv_ref[...] = jnp.zeros_like(dv_ref)
            dg_ref[...] = jnp.zeros_like(dg_ref)

        for h in range(mw // HEAD_DIM):
            sl = slice(h * HEAD_DIM, (h + 1) * HEAD_DIM)
            xh, r = _head_norm(q_ref[:, sl].astype(F32))
            qh = (xh * g_ref[...]).astype(BF16)
            kb = k_ref[:, sl].astype(BF16)
            vb = v_ref[:, sl].astype(BF16)
            dob = do_ref[:, sl].astype(BF16)
            s = lax.dot_general(qh, kb, (((1,), (1,)), ((), ())), preferred_element_type=F32) * scale
            e = jnp.exp(s - jnp.max(s, axis=-1, keepdims=True))
            p = e / jnp.sum(e, axis=-1, keepdims=True)
            dv_ref[:, sl] += lax.dot_general(p.astype(BF16), dob, (((0,), (0,)), ((), ())),
                                             preferred_element_type=F32)
            dp = lax.dot_general(dob, vb, (((1,), (1,)), ((), ())), preferred_element_type=F32)
            ds = (p * (dp - jnp.sum(dp * p, axis=-1, keepdims=True)) * scale).astype(BF16)
            dqh = jnp.dot(ds, kb, preferred_element_type=F32)
            dk_ref[:, sl] += lax.dot_general(ds, qh, (((0,), (0,)), ((), ())), preferred_element_type=F32)
            dg_ref[...] += jnp.sum(dqh * xh, axis=0, keepdims=True)
            dxh = dqh * g_ref[...]
            dq_ref[:, sl] = (r * (dxh - xh * jnp.mean(dxh * xh, axis=-1, keepdims=True))).astype(dq_ref.dtype)

    whole = lambda a: pl.BlockSpec(a.shape, lambda i: (0,) * a.ndim)
    g2 = gq.reshape(1, HEAD_DIM)
    kshape = jax.ShapeDtypeStruct((M, mw), F32)
    return pl.pallas_call(
        body, name=name, grid=(L // tl,),
        in_specs=[pl.BlockSpec((tl, mw), lambda i: (i, qcol)), whole(khat), whole(v), whole(g2),
                  pl.BlockSpec((tl, mw), lambda i: (i, dcol))],
        out_specs=[pl.BlockSpec((tl, mw), lambda i: (i, 0)), whole(kshape), whole(kshape),
                   pl.BlockSpec((1, HEAD_DIM), lambda i: (0, 0))],
        out_shape=[jax.ShapeDtypeStruct((L, mw), BF16), kshape, kshape, jax.ShapeDtypeStruct((1, HEAD_DIM), F32)],
        compiler_params=_params("arbitrary"),
    )(proj, khat, v, g2, dcat)


def _cmul(ar, ai, br, bi):
    return ar * br - ai * bi, ar * bi + ai * br


def _s5_discretize(log_dt, a_re, a_im):
    dt = jnp.exp(log_dt)
    mag = jnp.exp(a_re * dt)
    ab_re, ab_im = mag * jnp.cos(a_im * dt), mag * jnp.sin(a_im * dt)
    den = a_re * a_re + a_im * a_im
    inv_re, inv_im = a_re / den, -a_im / den
    f_re, f_im = _cmul(ab_re - 1.0, ab_im, inv_re, inv_im)
    return dt, ab_re, ab_im, inv_re, inv_im, f_re, f_im


def _s5_prep_fwd(log_dt, a_re, a_im, bt_re, bt_im, name):
    G, N = a_re.shape
    C = bt_re.shape[0]

    def body(ld_ref, ar_ref, ai_ref, br_ref, bi_ref, pr_ref, pi_ref, bbr_ref, bbi_ref):
        _, ab_re, ab_im, _, _, f_re, f_im = _s5_discretize(ld_ref[...], ar_ref[...], ai_ref[...])
        p_re, p_im = ab_re, ab_im
        for e in range(8):
            pr_ref[e] = p_re
            pi_ref[e] = p_im
            p_re, p_im = _cmul(p_re, p_im, ab_re, ab_im)
        for ch in range(C):
            bbr_ref[ch], bbi_ref[ch] = _cmul(f_re, f_im, br_ref[ch], bi_ref[ch])

    pw = jax.ShapeDtypeStruct((8, G, N), F32)
    bb = jax.ShapeDtypeStruct((C, G, N), F32)
    return pl.pallas_call(body, name=name, out_shape=[pw, pw, bb, bb], compiler_params=_params())(
        log_dt.reshape(G, 1), a_re, a_im, bt_re, bt_im)


def _s5_prep_bwd(log_dt, a_re, a_im, bt_re, bt_im, ga_re, ga_im, gbb_re, gbb_im, name):
    G, N = a_re.shape
    C = bt_re.shape[0]

    def body(ld_ref, ar_ref, ai_ref, br_ref, bi_ref, gar_ref, gai_ref, gbr_ref, gbi_ref,
             gld_ref, gare_ref, gaim_ref, gbtr_ref, gbti_ref):
        a_re_, a_im_ = ar_ref[...], ai_ref[...]
        dt, ab_re, ab_im, inv_re, inv_im, f_re, f_im = _s5_discretize(ld_ref[...], a_re_, a_im_)
        gf_re, gf_im = jnp.zeros((G, N), F32), jnp.zeros((G, N), F32)
        for ch in range(C):
            gbr, gbi = gbr_ref[ch], gbi_ref[ch]
            gbtr_ref[ch], gbti_ref[ch] = _cmul(f_re, -f_im, gbr, gbi)
            tr, ti = _cmul(br_ref[ch], -bi_ref[ch], gbr, gbi)
            gf_re, gf_im = gf_re + tr, gf_im + ti
        t_re, t_im = _cmul(gf_re, gf_im, inv_re, -inv_im)
        gab_re, gab_im = gar_ref[...] + t_re, gai_ref[...] + t_im
        u_re, u_im = _cmul(dt * ab_re, -dt * ab_im, gab_re, gab_im)
        fl_re, fl_im = _cmul(f_re, f_im, inv_re, inv_im)
        w_re, w_im = _cmul(fl_re, -fl_im, gf_re, gf_im)
        gare_ref[...] = u_re - w_re
        gaim_ref[...] = u_im - w_im
        la_re, la_im = _cmul(a_re_, a_im_, ab_re, ab_im)
        gdt = jnp.sum(la_re * gab_re + la_im * gab_im, axis=1, keepdims=True)
        gld_ref[...] = dt * gdt

    gn = jax.ShapeDtypeStruct((G, N), F32)
    bb = jax.ShapeDtypeStruct((C, G, N), F32)
    return pl.pallas_call(body, name=name,
                          out_shape=[jax.ShapeDtypeStruct((G, 1), F32), gn, gn, bb, bb],
                          compiler_params=_params())(
        log_dt.reshape(G, 1), a_re, a_im, bt_re, bt_im, ga_re, ga_im, gbb_re, gbb_im)


def _dot3(a, b, dims=(((1,), (0,)), ((), ()))):
    return lax.dot_general(a.astype(BF16), b.astype(BF16), dims, preferred_element_type=F32)


def _block_diag(t):
    nch, P, r, c = t.shape
    eye = jnp.eye(P, dtype=t.dtype)
    return (t[:, :, :, None, :] * eye[None, :, None, :, None]).reshape(nch, P * r, P * c)


def _block_diag_extract(m, P):
    nch, R, Cc = m.shape
    r, c = R // P, Cc // P
    m5 = m.reshape(nch, P, r, P, c)
    return jnp.stack([m5[:, k, :, k, :] for k in range(P)], axis=1)


def _s5_tables(pw_re, pw_im):
    pr = pw_re.reshape(8, -1)
    pi = pw_im.reshape(8, -1)
    row = jnp.arange(8)[:, None]
    tiles = []
    for d in (1, 2, 4):
        keep = row >= d
        tiles += [jnp.where(keep, pr[d - 1][None], 0.0), jnp.where(keep, pi[d - 1][None], 0.0)]
    tiles += [pr, pi]
    for d in (1, 2, 4):
        keep = row <= 7 - d
        tiles += [jnp.where(keep, pr[d - 1][None], 0.0), jnp.where(keep, -pi[d - 1][None], 0.0)]
    tiles += [pr[::-1], -pi[::-1]]
    return jnp.stack(tiles, axis=0)


def _scan_tile(sr, si, m, shifts):
    for n, sh in enumerate(shifts):
        rr, ri = pltpu.roll(sr, sh, 0), pltpu.roll(si, sh, 0)
        mr, mi = m[2 * n], m[2 * n + 1]
        sr, si = sr + mr * rr - mi * ri, si + mr * ri + mi * rr
    return sr, si


def _s5_rows(L):
    return _pick(L, 512, 8)


def _s5_fwd(proj, bd_re, bd_im, cd_re, cd_imn, tables, d_skip, tok_w, name):
    L = proj.shape[0]
    nch = tok_w // LANE
    ns = S5_PACK * S5_STATE
    tl = _s5_rows(L)
    nr = L // tl

    def body(u_ref, bdr_ref, bdi_ref, cdr_ref, cdi_ref, tab_ref, d_ref, y_ref, xr_ref, xi_ref, carry):
        r = pl.program_id(1)

        @pl.when(r == 0)
        def _():
            carry[...] = jnp.zeros_like(carry)

        u = u_ref[...]
        xr_ref[...] = _dot3(u, bdr_ref[0])
        xi_ref[...] = _dot3(u, bdi_ref[0])
        m = [tab_ref[n] for n in range(8)]

        def step(t, c):
            cr, ci = c
            rows = pl.ds(pl.multiple_of(t * 8, 8), 8)
            sr, si = _scan_tile(xr_ref[rows, :], xi_ref[rows, :], m, (1, 2, 4))
            sr, si = sr + m[6] * cr - m[7] * ci, si + m[6] * ci + m[7] * cr
            xr_ref[rows, :] = sr
            xi_ref[rows, :] = si
            return jnp.broadcast_to(sr[7:8, :], (8, ns)), jnp.broadcast_to(si[7:8, :], (8, ns))

        cr, ci = lax.fori_loop(0, tl // 8, step, (carry[0], carry[1]))
        carry[0] = cr
        carry[1] = ci
        y_ref[...] = _dot3(xr_ref[...], cdr_ref[0]) + _dot3(xi_ref[...], cdi_ref[0]) + d_ref[...] * u

    chunk3 = lambda a: pl.BlockSpec((1,) + a.shape[1:], lambda j, r: (j, 0, 0))
    return pl.pallas_call(
        body, name=name, grid=(nch, nr),
        in_specs=[pl.BlockSpec((tl, LANE), lambda j, r: (r, j)),
                  chunk3(bd_re), chunk3(bd_im), chunk3(cd_re), chunk3(cd_imn),
                  pl.BlockSpec((8, 8, ns), lambda j, r: (0, 0, j)),
                  pl.BlockSpec((1, LANE), lambda j, r: (0, j))],
        out_specs=[pl.BlockSpec((tl, LANE), lambda j, r: (r, j)),
                   pl.BlockSpec((tl, ns), lambda j, r: (r, j)),
                   pl.BlockSpec((tl, ns), lambda j, r: (r, j))],
        out_shape=[jax.ShapeDtypeStruct((L, tok_w), F32),
                   jax.ShapeDtypeStruct((L, nch * ns), F32), jax.ShapeDtypeStruct((L, nch * ns), F32)],
        scratch_shapes=[pltpu.VMEM((2, 8, ns), F32)],
        compiler_params=_params("parallel", "arbitrary"),
    )(proj, bd_re, bd_im, cd_re, cd_imn, tables, d_skip.reshape(1, tok_w))


def _s5_bwd(proj, dy, xr, xi, bd_re, bd_im, cd_re, cd_imn, tables, d_skip, tok_w, name):
    L = proj.shape[0]
    nch = tok_w // LANE
    ns = S5_PACK * S5_STATE
    tl = _s5_rows(L)
    nr = L // tl
    tn = (((0,), (0,)), ((), ()))
    nt = (((1,), (1,)), ((), ()))

    def body(u_ref, dy_ref, xr_ref, xi_ref, bdr_ref, bdi_ref, cdr_ref, cdi_ref, tab_ref, d_ref,
             du_ref, gcr_ref, gci_ref, gbr_ref, gbi_ref, gar_ref, gai_ref, gd_ref, lr, li, carry):
        r = pl.program_id(1)

        @pl.when(r == 0)
        def _():
            carry[...] = jnp.zeros_like(carry)
            for ref in (gcr_ref, gci_ref, gbr_ref, gbi_ref, gar_ref, gai_ref, gd_ref):
                ref[...] = jnp.zeros_like(ref)

        u, dyb = u_ref[...], dy_ref[...]
        lr[...] = _dot3(dyb, cdr_ref[0], nt)
        li[...] = _dot3(dyb, cdi_ref[0], nt)
        m = [tab_ref[n] for n in range(8)]
        last = lax.broadcasted_iota(jnp.int32, (8, ns), 0) == 7

        def step(it, c):
            cr, ci, ar, ai = c
            t = tl // 8 - 1 - it
            rows = pl.ds(pl.multiple_of(t * 8, 8), 8)
            sr, si = _scan_tile(lr[rows, :], li[rows, :], m, (7, 6, 4))
            sr, si = sr + m[6] * cr - m[7] * ci, si + m[6] * ci + m[7] * cr
            lr[rows, :] = sr
            li[rows, :] = si
            nr_, ni_ = jnp.where(last, cr, pltpu.roll(sr, 7, 0)), jnp.where(last, ci, pltpu.roll(si, 7, 0))
            xr_t, xi_t = xr_ref[rows, :], xi_ref[rows, :]
            ar = ar + xr_t * nr_ + xi_t * ni_
            ai = ai + xr_t * ni_ - xi_t * nr_
            return jnp.broadcast_to(sr[0:1, :], (8, ns)), jnp.broadcast_to(si[0:1, :], (8, ns)), ar, ai

        z = jnp.zeros((8, ns), F32)
        cr, ci, ar, ai = lax.fori_loop(0, tl // 8, step, (carry[0], carry[1], z, z))
        carry[0] = cr
        carry[1] = ci
        gar_ref[...] += jnp.sum(ar, axis=0, keepdims=True)
        gai_ref[...] += jnp.sum(ai, axis=0, keepdims=True)
        lam_r, lam_i = lr[...], li[...]
        du_ref[...] = (_dot3(lam_r, bdr_ref[0], nt) + _dot3(lam_i, bdi_ref[0], nt)
                       + d_ref[...] * dyb).astype(du_ref.dtype)
        gd_ref[...] += jnp.sum(dyb * u, axis=0, keepdims=True)
        gcr_ref[0] += _dot3(xr_ref[...], dyb, tn)
        gci_ref[0] += _dot3(xi_ref[...], dyb, tn)
        gbr_ref[0] += _dot3(u, lam_r, tn)
        gbi_ref[0] += _dot3(u, lam_i, tn)

    rev = lambda j, r: (nr - 1 - r, j)
    chunk3 = lambda a: pl.BlockSpec((1,) + a.shape[1:], lambda j, r: (j, 0, 0))
    gcd = jax.ShapeDtypeStruct((nch, ns, LANE), F32)
    gbd = jax.ShapeDtypeStruct((nch, LANE, ns), F32)
    gab = jax.ShapeDtypeStruct((1, nch * ns), F32)
    return pl.pallas_call(
        body, name=name, grid=(nch, nr),
        in_specs=[pl.BlockSpec((tl, LANE), rev), pl.BlockSpec((tl, LANE), rev),
                  pl.BlockSpec((tl, ns), rev), pl.BlockSpec((tl, ns), rev),
                  chunk3(bd_re), chunk3(bd_im), chunk3(cd_re), chunk3(cd_imn),
                  pl.BlockSpec((8, 8, ns), lambda j, r: (1, 0, j)),
                  pl.BlockSpec((1, LANE), lambda j, r: (0, j))],
        out_specs=[pl.BlockSpec((tl, LANE), rev), chunk3(gcd), chunk3(gcd), chunk3(gbd), chunk3(gbd),
                   pl.BlockSpec((1, ns), lambda j, r: (0, j)), pl.BlockSpec((1, ns), lambda j, r: (0, j)),
                   pl.BlockSpec((1, LANE), lambda j, r: (0, j))],
        out_shape=[jax.ShapeDtypeStruct((L, tok_w), BF16), gcd, gcd, gbd, gbd, gab, gab,
                   jax.ShapeDtypeStruct((1, tok_w), F32)],
        scratch_shapes=[pltpu.VMEM((tl, ns), F32), pltpu.VMEM((tl, ns), F32), pltpu.VMEM((2, 8, ns), F32)],
        compiler_params=_params("parallel", "arbitrary"),
    )(proj, dy, xr, xi, bd_re, bd_im, cd_re, cd_imn, tables, d_skip.reshape(1, tok_w))


def _lin(dev):
    return 4 * dev[0] + 2 * dev[1] + dev[2]


def _shard_view(ref, axis, index, size):
    ix = tuple(pl.ds(index * size, size) if d == axis else pl.ds(0, ref.shape[d]) for d in range(len(ref.shape)))
    return ref.at[ix]


def _slot_index(dev, paired):
    idx = _lin(dev)
    half = N_DEV // 2
    return 2 * (idx % half) + idx // half if paired else idx


def _comm_sems(n):
    return [pltpu.SemaphoreType.DMA((n, 7)), pltpu.SemaphoreType.DMA((n, 7)), pltpu.SemaphoreType.DMA((n,))]


def _gather_side(shards, axes, paired=None):
    n = len(shards)
    paired = [False] * n if paired is None else paired
    sizes = [s.shape[a] for s, a in zip(shards, axes)]

    def tools(ins, outs, sems):
        send_sems, recv_sems, local_sems = sems
        x, y, c = lax.axis_index("x"), lax.axis_index("y"), lax.axis_index("c")
        me, sibling = (x, y, c), (x, y, 1 - c)
        chips = [(1 - x, y), (x, 1 - y), (1 - x, 1 - y)]

        def slot(a, dev):
            return _shard_view(outs[a], axes[a], _slot_index(dev, paired[a]), sizes[a])

        def copy(a, k, block, to, src=None):
            return pltpu.make_async_remote_copy(
                src_ref=slot(a, block) if src is None else src, dst_ref=slot(a, block),
                send_sem=send_sems.at[a, k], recv_sem=recv_sems.at[a, k],
                device_id=to, device_id_type=MESH)

        mine = [pltpu.make_async_copy(ins[a], slot(a, me), local_sems.at[a]) for a in range(n)]
        first = []
        for a in range(n):
            first.append(copy(a, 0, me, sibling, src=ins[a]))
            first += [copy(a, 1 + j, me, (*chip, c), src=ins[a]) for j, chip in enumerate(chips)]
        return me, sibling, chips, c, copy, mine, first

    def start(ins, outs, sems):
        _, _, _, _, _, mine, first = tools(ins, outs, sems)
        for cp in mine + first:
            cp.start()

    def finish(ins, outs, sems):
        me, sibling, chips, c, copy, mine, first = tools(ins, outs, sems)
        passed = []
        for j, chip in enumerate(chips):
            for a in range(n):
                copy(a, 1 + j, (*chip, c), me).wait_recv()
                cp = copy(a, 4 + j, (*chip, c), sibling)
                cp.start()
                passed.append(cp)
        for a in range(n):
            copy(a, 0, sibling, me).wait_recv()
            for j, chip in enumerate(chips):
                copy(a, 4 + j, (*chip, 1 - c), me).wait_recv()
        for cp in first + passed:
            cp.wait_send()
        for cp in mine:
            cp.wait()

    out_shape = []
    for s, a in zip(shards, axes):
        shp = list(s.shape)
        shp[a] *= N_DEV
        out_shape.append(jax.ShapeDtypeStruct(tuple(shp), s.dtype))
    return _Side(list(shards), out_shape, _comm_sems(n), start, finish)


def _scatter_side(fulls, axes, paired=None):
    n = len(fulls)
    paired = [False] * n if paired is None else paired
    sizes = [f.shape[a] // N_DEV for f, a in zip(fulls, axes)]
    masks = [(mx, my, mc) for mx in (0, 1) for my in (0, 1) for mc in (0, 1)][1:]

    def copies(ins, outs, sems):
        send_sems, recv_sems, local_sems = sems
        x, y, c = lax.axis_index("x"), lax.axis_index("y"), lax.axis_index("c")
        me = (x, y, c)
        flip = lambda v, m: 1 - v if m else v
        piece = lambda a, dev: _shard_view(ins[a], axes[a], _slot_index(dev, paired[a]), sizes[a])
        local = [pltpu.make_async_copy(piece(a, me), outs[a].at[_lin(me)], local_sems.at[a]) for a in range(n)]
        sends = []
        for k, (mx, my, mc) in enumerate(masks):
            peer = (flip(x, mx), flip(y, my), flip(c, mc))
            for a in range(n):
                sends.append(pltpu.make_async_remote_copy(
                    src_ref=piece(a, peer), dst_ref=outs[a].at[_lin(me)],
                    send_sem=send_sems.at[a, k], recv_sem=recv_sems.at[a, k],
                    device_id=peer, device_id_type=MESH))
        return local, sends

    def start(ins, outs, sems):
        local, sends = copies(ins, outs, sems)
        for cp in local + sends:
            cp.start()

    def finish(ins, outs, sems):
        local, sends = copies(ins, outs, sems)
        for cp in sends:
            cp.wait()
        for cp in local:
            cp.wait()

    out_shape = []
    for f, a, sz in zip(fulls, axes, sizes):
        shp = list(f.shape)
        shp[a] = sz
        out_shape.append(jax.ShapeDtypeStruct((N_DEV,) + tuple(shp), f.dtype))
    return _Side(list(fulls), out_shape, _comm_sems(n), start, finish)


def _all_gather(shards, axes, name, paired=None):
    return _run_side(_gather_side(shards, axes, paired), name)


def _adamw_sum(w, m, v, parts, name, layer=0, prev=None):
    rows, cols = parts.shape[1:]
    n_layers = w.shape[0] // rows
    c1 = 1.0 / (1.0 - ADAM_B1 ** ADAM_STEP)
    c2 = 1.0 / (1.0 - ADAM_B2 ** ADAM_STEP)

    def fn(j, wb, mb, vb, *ps):
        g = ps[0].astype(F32)
        for p in ps[1:]:
            g = g + p.astype(F32)
        mn = ADAM_B1 * mb + (1.0 - ADAM_B1) * g
        vn = ADAM_B2 * vb + (1.0 - ADAM_B2) * (g * g)
        delta = -ADAM_LR * ((mn * c1) / (jnp.sqrt(vn * c2) + ADAM_EPS) + ADAM_WD * wb)
        return g, delta, mn, vn

    per_row = cols * (12 + 16 + N_DEV * parts.dtype.itemsize)
    tl = _row_tile(rows, per_row, budget=12 * 1024 * 1024)
    flat = parts.reshape(N_DEV * rows, cols)
    nblk = rows // tl
    ins = [(a, layer * nblk, 0) for a in (w, m, v)] + [(flat, d * nblk, 0) for d in range(N_DEV)]
    return _ew(fn, ins, [(cols, F32)] * 4, rows=rows, cols=cols, tl=tl, name=name,
               into=(n_layers * rows, layer * nblk, prev))


_NAMES = ['ffn1_norm', 'ffn1_w_gu', 'ffn1_w_down', 'mix_norm', 'mem_norm', 'w_mem_kv', 'xq_norm', 'xk_norm',
          'w_out', 'ffn2_norm', 'ffn2_w_gu', 'ffn2_w_down', 'sb_w_in', 's5_w_in', 's5_log_dt', 's5_a_re',
          's5_a_im', 's5_b_re', 's5_b_im', 's5_c_re', 's5_c_im', 's5_d', 's5_w_glu']
_SHARDED = {'ffn1_w_gu': 1, 'ffn1_w_down': 0, 'w_mem_kv': 0, 'w_out': 0, 'ffn2_w_gu': 1, 'ffn2_w_down': 0,
            'sb_w_in': 1, 's5_w_in': 0, 's5_w_glu': 0}
_PER_LAYER = ['ffn1_w_gu', 'ffn1_w_down', 'w_mem_kv', 'w_out', 'ffn2_w_gu', 'ffn2_w_down']
_SMALL = ['ffn1_norm', 'mix_norm', 'mem_norm', 'xq_norm', 'xk_norm', 'ffn2_norm', 's5_log_dt', 's5_a_re',
          's5_a_im', 's5_b_re', 's5_b_im', 's5_c_re', 's5_c_im']


def _ffn_fwd(x, gain, w_gu, w_down, tag, side=None, side_down=None):
    h = _rmsnorm_fwd(x, gain, f"{tag}_norm")
    gu, s, extra = _mm_swiglu(h, w_gu, f"{tag}_gu", side)
    out, extra_down = _mm(s, w_down, res=x, scale=0.5, name=f"{tag}_down", side=side_down), ()
    if side_down is not None:
        out, extra_down = out
    return out, (x, h, gu, s), extra, extra_down


def _ffn_bwd(dout, dout16, saved, gain, w_gu, w_down, tag):
    x, h, gu, s = saved
    g_down = _mm(s, dout16, ta=True, scale=0.5, out_dtype=BF16, name=f"{tag}_gdown")
    dgu, _ = _mm_dswiglu(dout16, w_down, gu, f"{tag}_ds")
    g_gu, (p_down,) = _mm(h, dgu, ta=True, out_dtype=BF16, name=f"{tag}_ggu", side=_scatter_side([g_down], [0]))
    dh, (p_gu,) = _mm(dgu, w_gu, tb=True, name=f"{tag}_dh", side=_scatter_side([g_gu], [1], [True]))
    dx, dx16, g_gain = _rmsnorm_bwd(dh, x, gain, dout, f"{tag}_dnorm")
    return dx, dx16, g_gain, p_gu, p_down


def _mem_fwd(mem, mem_gain, w_kv, gk, tag):
    mem_h = _rmsnorm_fwd(mem, mem_gain, f"{tag}_memnorm")
    kv = _mm(mem_h, w_kv, name=f"{tag}_memkv")
    khat, v = _memkv_fwd(kv, gk, f"{tag}_memk")
    return mem_h, kv, khat, v


def _s5_arrange(p, j, nch):
    bt_re = jnp.transpose(p['s5_b_re'][j], (2, 0, 1))
    bt_im = jnp.transpose(p['s5_b_im'][j], (2, 0, 1))
    return bt_re, bt_im


def _mixer_fwd(i, x, p, W, mem, tag, side=None):
    L, D = x.shape
    mw = D // 4
    tok_w = D - mw
    j = i // 2
    h = _rmsnorm_fwd(x, p['mix_norm'][i], f"{tag}_norm")
    mem_h, kv, khat, v = _mem_fwd(mem, p['mem_norm'][i], W['w_mem_kv'], p['xk_norm'][i], tag)
    sv = dict(x=x, h=h, mem_h=mem_h, kv=kv, khat=khat, v=v)
    proj = _mm(h, W['sb_w_in' if i % 2 == 0 else 's5_w_in'], out_dtype=BF16 if i % 2 == 0 else F32,
               name=f"{tag}_in", side=side)
    extra = ()
    if side is not None:
        proj, extra = proj
    if i % 2 == 0:
        tok, rowsum = _sb_fwd(proj, tok_w // HEAD_DIM, f"{tag}_sb")
        sv.update(rowsum=rowsum)
    else:
        nch = tok_w // LANE
        bt_re, bt_im = _s5_arrange(p, j, nch)
        pw_re, pw_im, bb_re, bb_im = _s5_prep_fwd(p['s5_log_dt'][j], p['s5_a_re'][j], p['s5_a_im'][j],
                                                  bt_re, bt_im, f"{tag}_s5prep")
        C, N = bb_re.shape[0], bb_re.shape[2]
        bd_re = _block_diag(jnp.swapaxes(bb_re, 0, 1).reshape(nch, S5_PACK, C, N))
        bd_im = _block_diag(jnp.swapaxes(bb_im, 0, 1).reshape(nch, S5_PACK, C, N))
        ct_re = jnp.swapaxes(p['s5_c_re'][j], 1, 2).reshape(nch, S5_PACK, N, C)
        ct_im = jnp.swapaxes(p['s5_c_im'][j], 1, 2).reshape(nch, S5_PACK, N, C)
        cd_re, cd_imn = _block_diag(ct_re), _block_diag(-ct_im)
        tables = _s5_tables(pw_re, pw_im)
        y, xr, xi = _s5_fwd(proj, bd_re, bd_im, cd_re, cd_imn, tables, W['s5_d'][j], tok_w, f"{tag}_s5")
        tl = _row_tile(L, tok_w * 12)
        yg = _ew(lambda jj, yb: (_gelu(yb),), [y], [(tok_w, BF16)], rows=L, cols=tok_w, tl=tl, name=f"{tag}_gelu")[0]
        pre = _mm(yg, W['s5_w_glu'], name=f"{tag}_glu")
        tok = _ew(lambda jj, yb, pb: (_gelu(yb) * jax.nn.sigmoid(pb),), [y, pre], [(tok_w, BF16)],
                  rows=L, cols=tok_w, tl=tl, name=f"{tag}_gate")[0]
        sv.update(y=y, xr=xr, xi=xi, pre=pre, yg=yg, bt_re=bt_re, bt_im=bt_im, bd_re=bd_re, bd_im=bd_im,
                  cd_re=cd_re, cd_imn=cd_imn, tables=tables)
    cross = _memattn_fwd(proj, khat, v, p['xq_norm'][i], f"{tag}_mem")
    cat = jnp.concatenate([tok.astype(BF16), cross.astype(BF16)], axis=1)
    out = _mm(cat, W['w_out'], res=x, name=f"{tag}_out")
    sv.update(proj=proj, cat=cat)
    return out, sv, extra


def _mixer_bwd(i, dout, dout16, sv, p, W, mem, tag):
    x, h, proj, cat = sv['x'], sv['h'], sv['proj'], sv['cat']
    L, D = x.shape
    mw = D // 4
    tok_w = D - mw
    j = i // 2
    g, parts = {}, {}
    g_out = _mm(cat, dout16, ta=True, out_dtype=BF16, name=f"{tag}_gout")
    dcat, (parts['w_out'],) = _mm(dout16, W['w_out'], tb=True, name=f"{tag}_dcat",
                                  side=_scatter_side([g_out], [_SHARDED['w_out']]))
    dqm, dkhat, dv, g_xq = _memattn_bwd(proj, sv['khat'], sv['v'], p['xq_norm'][i], dcat, f"{tag}_dmem")
    g['xq_norm'] = g_xq
    dkv, g_xk = _memkv_bwd(sv['kv'], p['xk_norm'][i], dkhat, dv, f"{tag}_dmemk")
    g['xk_norm'] = g_xk
    late = {'w_mem_kv': _mm(sv['mem_h'], dkv, ta=True, out_dtype=BF16, name=f"{tag}_gmemkv")}
    dmem_h = _mm(dkv, W['w_mem_kv'], tb=True, name=f"{tag}_dmemh")
    _, _, g['mem_norm'] = _rmsnorm_bwd(dmem_h, mem, p['mem_norm'][i], None, f"{tag}_dmemnorm")
    if i % 2 == 0:
        dq, dk, dvv = _sb_bwd(proj, dcat, sv['rowsum'], tok_w // HEAD_DIM, f"{tag}_dsb")
        dproj = jnp.concatenate([dq.astype(BF16), dk.astype(BF16), dvv.astype(BF16), dqm.astype(BF16)], axis=1)
        w_in, in_name = W['sb_w_in'], 'sb_w_in'
    else:
        y, pre, yg = sv['y'], sv['pre'], sv['yg']
        tl = _row_tile(L, tok_w * 20)

        def gate_bwd(jj, dt, yb, pb):
            gl = _gelu(yb)
            sg = jax.nn.sigmoid(pb)
            return dt * gl * sg * (1.0 - sg)

        dpre = _ew(gate_bwd, [dcat, y, pre], [(tok_w, BF16)], rows=L, cols=tok_w, tl=tl, name=f"{tag}_dgate")[0]
        late['s5_w_glu'] = _mm(yg, dpre, ta=True, out_dtype=BF16, name=f"{tag}_gglu")
        dyg = _mm(dpre, W['s5_w_glu'], tb=True, name=f"{tag}_dyg")

        def gelu_bwd(jj, dt, yb, pb, db):
            return (dt * jax.nn.sigmoid(pb) + db) * _gelu_grad(yb)

        dy = _ew(gelu_bwd, [dcat, y, pre, dyg], [(tok_w, F32)], rows=L, cols=tok_w, tl=tl, name=f"{tag}_dgelu")[0]
        du, gcd_re, gcd_imn, gbd_re, gbd_im, ga_re, ga_im, g_d = _s5_bwd(
            proj, dy, sv['xr'], sv['xi'], sv['bd_re'], sv['bd_im'], sv['cd_re'], sv['cd_imn'], sv['tables'],
            W['s5_d'][j], tok_w, f"{tag}_ds5")
        G, N = p['s5_a_re'].shape[1], p['s5_a_re'].shape[2]
        C = S5_GROUP
        gct_re = _block_diag_extract(gcd_re, S5_PACK).reshape(G, N, C)
        gct_im = -_block_diag_extract(gcd_imn, S5_PACK).reshape(G, N, C)
        g['s5_c_re'] = jnp.swapaxes(gct_re, 1, 2)
        g['s5_c_im'] = jnp.swapaxes(gct_im, 1, 2)
        gbb_re = jnp.swapaxes(_block_diag_extract(gbd_re, S5_PACK).reshape(G, C, N), 0, 1)
        gbb_im = jnp.swapaxes(_block_diag_extract(gbd_im, S5_PACK).reshape(G, C, N), 0, 1)
        g_ld, g_are, g_aim, gbt_re, gbt_im = _s5_prep_bwd(
            p['s5_log_dt'][j], p['s5_a_re'][j], p['s5_a_im'][j], sv['bt_re'], sv['bt_im'],
            ga_re.reshape(G, N), ga_im.reshape(G, N), gbb_re, gbb_im, f"{tag}_ds5prep")
        g['s5_log_dt'] = g_ld.reshape(G)
        g['s5_a_re'], g['s5_a_im'] = g_are, g_aim
        g['s5_b_re'], g['s5_b_im'] = jnp.transpose(gbt_re, (1, 2, 0)), jnp.transpose(gbt_im, (1, 2, 0))
        g['s5_d'] = g_d.reshape(tok_w)
        dproj = jnp.concatenate([du.astype(BF16), dqm.astype(BF16)], axis=1)
        w_in, in_name = W['s5_w_in'], 's5_w_in'
    names = list(late)
    g_in, bufs = _mm(h, dproj, ta=True, out_dtype=BF16, name=f"{tag}_gin",
                     side=_scatter_side([late[n] for n in names], [_SHARDED[n] for n in names]))
    parts.update(zip(names, bufs))
    dh, (parts[in_name],) = _mm(dproj, w_in, tb=True, name=f"{tag}_dh",
                                side=_scatter_side([g_in], [_SHARDED[in_name]]))
    dx, dx16, g['mix_norm'] = _rmsnorm_bwd(dh, x, p['mix_norm'][i], dout, f"{tag}_dnorm")
    return dx, dx16, g, parts


def _pack(arrs):
    flat = jnp.concatenate([a.reshape(-1).astype(F32) for a in arrs])
    pad = (-flat.shape[0]) % (512 * LANE)
    flat = jnp.pad(flat, (0, pad))
    return flat.reshape(-1, LANE)


def _unpack(flat, like):
    flat = flat.reshape(-1)
    out, off = [], 0
    for a in like:
        n = math.prod(a.shape)
        out.append(flat[off:off + n].reshape(a.shape))
        off += n
    return out


def kernel(x, mem, ffn1_norm, ffn1_w_gu, ffn1_w_down, mix_norm, mem_norm, w_mem_kv, xq_norm, xk_norm, w_out, ffn2_norm, ffn2_w_gu, ffn2_w_down, sb_w_in, s5_w_in, s5_log_dt, s5_a_re, s5_a_im, s5_b_re, s5_b_im, s5_c_re, s5_c_im, s5_d, s5_w_glu, loss_target, m_ffn1_norm, m_ffn1_w_gu, m_ffn1_w_down, m_mix_norm, m_mem_norm, m_w_mem_kv, m_xq_norm, m_xk_norm, m_w_out, m_ffn2_norm, m_ffn2_w_gu, m_ffn2_w_down, m_sb_w_in, m_s5_w_in, m_s5_log_dt, m_s5_a_re, m_s5_a_im, m_s5_b_re, m_s5_b_im, m_s5_c_re, m_s5_c_im, m_s5_d, m_s5_w_glu, v_ffn1_norm, v_ffn1_w_gu, v_ffn1_w_down, v_mix_norm, v_mem_norm, v_w_mem_kv, v_xq_norm, v_xk_norm, v_w_out, v_ffn2_norm, v_ffn2_w_gu, v_ffn2_w_down, v_sb_w_in, v_s5_w_in, v_s5_log_dt, v_s5_a_re, v_s5_a_im, v_s5_b_re, v_s5_b_im, v_s5_c_re, v_s5_c_im, v_s5_d, v_s5_w_glu):
    given = dict(locals())
    p = {n: given[n] for n in _NAMES}
    mom = {n: given["m_" + n] for n in _NAMES}
    var = {n: given["v_" + n] for n in _NAMES}
    depth = ffn1_norm.shape[0]
    xs, mems, target = x[0], mem[0], loss_target[0]
    L, D = xs.shape
    me = 4 * lax.axis_index("x") + 2 * lax.axis_index("y") + lax.axis_index("c")

    def layer_names(i):
        return _PER_LAYER + (['sb_w_in'] if i % 2 == 0 else ['s5_w_in', 's5_w_glu'])

    n_b = s5_d.shape[0]
    d_rows = jnp.broadcast_to(s5_d.reshape(1, -1), (8, s5_d.size))
    d_full = _all_gather([d_rows], [0], "gather_s5d")[0][::8]
    d_full = jnp.swapaxes(d_full.reshape(N_DEV, n_b, -1), 0, 1).reshape(n_b, -1)

    def gather_group(i, names):
        shards = [_cast_bf16(p[nm], i // 2 if nm in ('sb_w_in', 's5_w_in', 's5_w_glu') else i, f"cast_{nm}")
                  for nm in names]
        return _gather_side(shards, [_SHARDED[nm] for nm in names], [nm.endswith('w_gu') for nm in names])

    groups = lambda i: (['ffn1_w_gu', 'ffn1_w_down'], [nm for nm in layer_names(i) if 'ffn' not in nm],
                        ['ffn2_w_gu', 'ffn2_w_down'])
    first, rest = groups(0)[0], groups(0)[1] + groups(0)[2]
    W = dict(zip(first, _run_side(gather_group(0, first), "gather_first")))
    Ws, saves = [], []
    act = xs
    for i in range(depth):
        W['s5_d'] = d_full
        Ws.append(W)
        nxt = [gather_group(i + 1, names) if i + 1 < depth else None for names in groups(i + 1)]
        if i == 0:
            act, s1, got, e1 = _ffn_fwd(act, p['ffn1_norm'][i], W['ffn1_w_gu'], W['ffn1_w_down'], "ffn1",
                                        gather_group(0, rest), nxt[0])
            W.update(zip(rest, got))
        else:
            act, s1, e1, _ = _ffn_fwd(act, p['ffn1_norm'][i], W['ffn1_w_gu'], W['ffn1_w_down'], "ffn1", nxt[0])
        act, s2, e2 = _mixer_fwd(i, act, p, W, mems, f"mix{i % 2}", nxt[1])
        act, s3, e3, _ = _ffn_fwd(act, p['ffn2_norm'][i], W['ffn2_w_gu'], W['ffn2_w_down'], "ffn2", nxt[2])
        saves.append((s1, s2, s3))
        W = {}
        if i + 1 < depth:
            for names, got in zip(groups(i + 1), (e1, e2, e3)):
                W.update(zip(names, got))

    loss_part, dact, dact16 = _loss_fwd_bwd(act, target, "loss")
    loss = lax.psum(loss_part[0, 0], ("x", "y", "c"))

    small_g = {n: [None] * p[n].shape[0] for n in _SMALL + ['s5_d']}
    big = {n: None for n in _SHARDED}
    for i in reversed(range(depth)):
        W = Ws[i]
        s1, s2, s3 = saves[i]
        j = i // 2
        parts = {}
        dact, dact16, gg, parts['ffn2_w_gu'], parts['ffn2_w_down'] = _ffn_bwd(
            dact, dact16, s3, p['ffn2_norm'][i], W['ffn2_w_gu'], W['ffn2_w_down'], "ffn2")
        small_g['ffn2_norm'][i] = gg
        dact, dact16, gm, pm = _mixer_bwd(i, dact, dact16, s2, p, W, mems, f"mix{i % 2}")
        parts.update(pm)
        dact, dact16, gg, parts['ffn1_w_gu'], parts['ffn1_w_down'] = _ffn_bwd(
            dact, dact16, s1, p['ffn1_norm'][i], W['ffn1_w_gu'], W['ffn1_w_down'], "ffn1")
        small_g['ffn1_norm'][i] = gg
        for n in _SMALL + ['s5_d']:
            if n in gm:
                small_g[n][j if n.startswith('s5_') else i] = gm[n]
        for nm in layer_names(i):
            li = j if nm in ('sb_w_in', 's5_w_in', 's5_w_glu') else i
            flat2 = lambda a: a.reshape(-1, a.shape[-1])
            big[nm] = _adamw_sum(flat2(p[nm]), flat2(mom[nm]), flat2(var[nm]), parts[nm], f"adamw_{nm}",
                                 layer=li, prev=big[nm])

    small_names = _SMALL + ['s5_d']
    g_small = [jnp.stack([gi.reshape(p[n].shape[1:]) if n != 's5_d' else gi for gi in small_g[n]])
               for n in small_names]
    d_w = jnp.zeros_like(d_full)
    cw = s5_d.shape[1]
    w_small = [p[n] for n in _SMALL]
    m_small = [mom[n] for n in _SMALL]
    v_small = [var[n] for n in _SMALL]
    place = lambda a: lax.dynamic_update_slice(d_w, a, (0, me * cw))
    flat_g = _pack(g_small)
    flat_w = _pack(w_small + [place(s5_d)])
    flat_m = _pack(m_small + [place(mom['s5_d'])])
    flat_v = _pack(v_small + [place(var['s5_d'])])
    parts = _all_gather([flat_g.reshape((1,) + flat_g.shape)], [0], "gather_small")[0]
    res_small = _adamw_sum(flat_w, flat_m, flat_v, parts, "adamw_small")
    like = [p[n] for n in _SMALL] + [d_full]
    un = [_unpack(r, like) for r in res_small]

    def result(kind, n):
        if n in _SHARDED:
            return big[n][kind].reshape(p[n].shape)
        if n == 's5_d':
            return lax.dynamic_slice(un[kind][-1], (0, me * cw), (n_b, cw))
        return un[kind][_SMALL.index(n)]

    outs = [loss, dact.reshape((1,) + dact.shape)]
    for kind in range(4):
        outs += [result(kind, n) for n in _NAMES]
    return tuple(outs)
```
